```python
import math
import jax
import jax.numpy as jnp
from jax import lax
import numpy as np

D_MODEL = 1024
BATCH = 8
SEQ = 2048
DEPTH = 2

N_META = 16
HG_HEADS = 8
HG_KDIM = 128
HG_VDIM = 128
HG_WIDTH = HG_HEADS * HG_VDIM
HG_QK_WIDTH = HG_HEADS * HG_KDIM
CHUNK = 16
AT_HEADS = 8
AT_HEAD_DIM = 64
AT_WIDTH = AT_HEADS * 2 * AT_HEAD_DIM
MIX_WIDTH = HG_WIDTH + AT_WIDTH
Q_BLOCK = 128
ROPE_THETA = 10000.0
NORM_EPS = 1e-6
F_FLOOR = 1e-30
IN_COLS = (HG_QK_WIDTH, HG_QK_WIDTH, HG_QK_WIDTH, HG_WIDTH, HG_WIDTH,
           AT_WIDTH, AT_WIDTH, AT_WIDTH, AT_WIDTH)
IN_WIDTH = 9216

kernel_name = "hymba_hgrn2_diffattn_encoder"


def rmsnorm(x, g):
    xf = x.astype(jnp.float32)
    ms = jnp.mean(xf * xf, axis=-1, keepdims=True)
    return xf * lax.rsqrt(ms + NORM_EPS) * g.astype(jnp.float32)


def rope_tables(L):
    inv = ROPE_THETA ** (-jnp.arange(0, AT_HEAD_DIM, 2, dtype=jnp.float32) / AT_HEAD_DIM)
    ang = jnp.arange(L, dtype=jnp.float32)[:, None] * inv[None, :]
    return jnp.cos(ang), jnp.sin(ang)


def apply_rope(t, cos, sin):
    tf = t.astype(jnp.float32)
    half = AT_HEAD_DIM // 2
    t1, t2 = tf[..., :half], tf[..., half:]
    c = cos[None, :, None, None, :]
    s = sin[None, :, None, None, :]
    return jnp.concatenate([t1 * c - t2 * s, t2 * c + t1 * s], axis=-1)


def hgrn2_chunk_scan(q, k, v, log_f):
    Bs, L, H, K = q.shape
    V = v.shape[-1]
    N = L // CHUNK
    q, k, log_f = (t.reshape(Bs, N, CHUNK, H, K) for t in (q, k, log_f))
    v = v.reshape(Bs, N, CHUNK, H, V)
    b = jnp.cumsum(log_f, axis=2)
    b_last = b[:, :, -1:]
    mask = jnp.tril(jnp.ones((CHUNK, CHUNK), dtype=bool))[:, :, None, None]
    diff = b[:, :, :, None] - b[:, :, None, :]
    dec = jnp.where(mask, jnp.exp(jnp.where(mask, diff, 0.0)), 0.0)
    att = jnp.einsum('bnthk,bnshk,bntshk->bnhts', q, k, dec)
    o_intra = jnp.einsum('bnhts,bnshv->bnthv', att, v)
    q_dec = q * jnp.exp(b)
    k_dec = k * jnp.exp(b_last - b)
    decay = jnp.exp(b_last[:, :, 0])

    def step(S, xs):
        qd, kd, vn, dn = xs
        o = jnp.einsum('bthk,bhkv->bthv', qd, S)
        S = dn[..., None] * S + jnp.einsum('bthk,bthv->bhkv', kd, vn)
        return S, o

    xs = (jnp.moveaxis(q_dec, 1, 0), jnp.moveaxis(k_dec, 1, 0),
          jnp.moveaxis(v, 1, 0), jnp.moveaxis(decay, 1, 0))
    S0 = jnp.zeros((Bs, H, K, V), jnp.float32)
    _, o_inter = lax.scan(step, S0, xs)
    o = o_intra + jnp.moveaxis(o_inter, 0, 1)
    return o.reshape(Bs, L, H, V)


def hgrn2_mixer(q, fz_fwd, fz_bwd, i_v, gate, lb, norm_gain):
    Bs, L = q.shape[:2]
    qh = q.reshape(Bs, L, HG_HEADS, HG_KDIM).astype(jnp.float32)
    vh = i_v.reshape(Bs, L, HG_HEADS, HG_VDIM).astype(jnp.float32)

    def gates(z, lb_dir):
        z = z.reshape(Bs, L, HG_HEADS, HG_KDIM).astype(jnp.float32)
        lb_dir = lb_dir.reshape(HG_HEADS, HG_KDIM)
        f = lb_dir + (1.0 - lb_dir) * jax.nn.sigmoid(z)
        k_in = (1.0 - lb_dir) * jax.nn.sigmoid(-z)
        log_f = jnp.log(jnp.maximum(f, F_FLOOR))
        return k_in, log_f

    k_f, a_f = gates(fz_fwd, lb[0])
    k_b, a_b = gates(fz_bwd, lb[1])
    flip = lambda t: jnp.flip(t, axis=1)
    o = hgrn2_chunk_scan(qh, k_f, vh, a_f) + \
        flip(hgrn2_chunk_scan(flip(qh), flip(k_b), flip(vh), flip(a_b)))
    g = gate.reshape(Bs, L, HG_HEADS, HG_VDIM).astype(jnp.float32)
    o = rmsnorm(o, norm_gain) * jax.nn.silu(g)
    return o.reshape(Bs, L, HG_WIDTH)


def diff_attention_mixer(q, k, v, gate, lam_vecs, subln_g, layer_idx, cos, sin):
    Bs, L = q.shape[:2]
    qh = apply_rope(q.reshape(Bs, L, AT_HEADS, 2, AT_HEAD_DIM), cos, sin)
    kh = apply_rope(k.reshape(Bs, L, AT_HEADS, 2, AT_HEAD_DIM), cos, sin)
    vh = v.reshape(Bs, L, AT_HEADS, 2 * AT_HEAD_DIM).astype(jnp.float32)
    lam_init = 0.8 - 0.6 * math.exp(-0.3 * layer_idx)
    lv = lam_vecs.astype(jnp.float32)
    lam = jnp.exp(jnp.sum(lv[0] * lv[1])) - jnp.exp(jnp.sum(lv[2] * lv[3])) + lam_init
    nb = -(-L // Q_BLOCK)
    pad = nb * Q_BLOCK - L
    qp = jnp.pad(qh, ((0, 0), (0, pad), (0, 0), (0, 0), (0, 0)))
    qb = qp.reshape(Bs, nb, Q_BLOCK, AT_HEADS, 2, AT_HEAD_DIM).transpose(1, 0, 2, 3, 4, 5)
    scale = AT_HEAD_DIM ** -0.5

    def block(qblk):
        s = jnp.einsum('bqhcd,bkhcd->bhcqk', qblk, kh) * scale
        p = jax.nn.softmax(s, axis=-1)
        p_diff = p[:, :, 0] - lam * p[:, :, 1]
        return jnp.einsum('bhqk,bkhv->bqhv', p_diff, vh)

    o = lax.map(block, qb)
    o = o.transpose(1, 0, 2, 3, 4).reshape(Bs, nb * Q_BLOCK, AT_HEADS, 2 * AT_HEAD_DIM)[:, :L]
    g = gate.reshape(Bs, L, AT_HEADS, 2 * AT_HEAD_DIM).astype(jnp.float32)
    o = rmsnorm(o, subln_g) * (1.0 - lam_init) * jax.nn.silu(g)
    return o.reshape(Bs, L, AT_WIDTH)


def setup_inputs(seed: int = 0) -> dict:
    key = jax.random.key(seed)
    ks = jax.random.split(key, 10)
    f32 = jnp.float32
    return {
        "x": jax.random.normal(ks[0], (BATCH, SEQ, D_MODEL), f32),
        "meta_tokens": jax.random.normal(ks[1], (N_META, D_MODEL), f32),
        "norm_g": 1.0 + 0.02 * jax.random.normal(ks[2], (DEPTH, D_MODEL), f32),
        "w_in": jax.random.normal(ks[3], (DEPTH, D_MODEL, IN_WIDTH), f32) * D_MODEL ** -0.5,
        "hg_lb_logits": jax.random.normal(ks[4], (DEPTH, 2, HG_QK_WIDTH), f32),
        "hg_norm_g": 1.0 + 0.02 * jax.random.normal(ks[5], (DEPTH, HG_VDIM), f32),
        "diff_lambda": 0.1 * jax.random.normal(ks[6], (DEPTH, 4, AT_HEAD_DIM), f32),
        "diff_subln_g": 1.0 + 0.02 * jax.random.normal(ks[7], (DEPTH, 2 * AT_HEAD_DIM), f32),
        "w_out": jax.random.normal(ks[8], (DEPTH, MIX_WIDTH, D_MODEL), f32) * MIX_WIDTH ** -0.5,
        "final_norm_g": 1.0 + 0.02 * jax.random.normal(ks[9], (D_MODEL,), f32),
    }


def reference(x, meta_tokens, norm_g, w_in, hg_lb_logits, hg_norm_g, diff_lambda,
              diff_subln_g, w_out, final_norm_g):
    dtype = x.dtype
    Bs = x.shape[0]
    meta = jnp.broadcast_to(meta_tokens[None].astype(dtype), (Bs, N_META, D_MODEL))
    h = jnp.concatenate([meta, x], axis=1)
    L = h.shape[1]
    cos, sin = rope_tables(L)
    lbp = jax.nn.softmax(hg_lb_logits.astype(jnp.float32), axis=0)
    lbs = jnp.cumsum(lbp, axis=0) - lbp[0:1]
    split_idx = [int(s) for s in np.cumsum(IN_COLS)[:-1]]
    for l in range(DEPTH):
        u = rmsnorm(h, norm_g[l]).astype(dtype)
        p = u @ w_in[l]
        hq, hf_f, hf_b, hi, hgt, aq, ak, av, agt = jnp.split(p, split_idx, axis=-1)
        o_h = hgrn2_mixer(hq, hf_f, hf_b, hi, hgt, lbs[l], hg_norm_g[l])
        o_a = diff_attention_mixer(aq, ak, av, agt, diff_lambda[l], diff_subln_g[l], l, cos, sin)
        mixed = jnp.concatenate([o_h, o_a], axis=-1).astype(dtype)
        h = h + mixed @ w_out[l]
    out = rmsnorm(h, final_norm_g).astype(dtype)
    return out[:, N_META:]
```

```python
import functools
import math

import numpy as np
import jax
import jax.numpy as jnp
from jax import lax
from jax.experimental import pallas as pl
from jax.experimental.pallas import tpu as pltpu

F32 = jnp.float32
BF16 = jnp.bfloat16

N_META = 16
HG_HEADS = 8
HG_KDIM = 128
HG_VDIM = 128
AT_HEADS = 8
AT_HEAD_DIM = 64
ROPE_THETA = 10000.0
NORM_EPS = 1e-6
F_FLOOR = 1e-30
N_GROUPS = 9

LANES = 128
SUBLANES = 8
VMEM_LIMIT = 56 * 1024 * 1024

HG_CHUNK = 128
HG_LEVELS = (64, 32, 16, 8, 4, 2, 1)
HG_DIAG_LEVEL = len(HG_LEVELS)


def _pick_tile(n, cap, mult):
    best = None
    for t in range(mult, min(n, cap) + 1, mult):
        if n % t == 0:
            best = t
    assert best is not None, (n, cap, mult)
    return best


def _dot(a, b):
    return jnp.dot(a, b, preferred_element_type=F32)


def _dot_nt(a, b):
    return lax.dot_general(a, b, (((1,), (1,)), ((), ())), preferred_element_type=F32)


def _norm_inproj_kernel(h_ref, g_ref, w_ref, p_ref, u_ref):
    @pl.when(pl.program_id(1) == 0)
    def _():
        x = h_ref[...]
        ms = jnp.mean(x * x, axis=-1, keepdims=True)
        u_ref[...] = (x * lax.rsqrt(ms + NORM_EPS) * g_ref[...]).astype(BF16)

    p_ref[...] = _dot(u_ref[...], w_ref[...]).astype(BF16)


def _norm_inproj(h2d, g, w_bf16):
    m, d = h2d.shape
    n = w_bf16.shape[1]
    tm = _pick_tile(m, 512, LANES)
    tn = _pick_tile(n, 1024, LANES)
    return pl.pallas_call(
        _norm_inproj_kernel,
        out_shape=jax.ShapeDtypeStruct((m, n), BF16),
        grid=(m // tm, n // tn),
        in_specs=[
            pl.BlockSpec((tm, d), lambda i, j: (i, 0)),
            pl.BlockSpec((1, d), lambda i, j: (0, 0)),
            pl.BlockSpec((d, tn), lambda i, j: (0, j)),
        ],
        out_specs=pl.BlockSpec((tm, tn), lambda i, j: (i, j)),
        scratch_shapes=[pltpu.VMEM((tm, d), BF16)],
        compiler_params=pltpu.CompilerParams(
            dimension_semantics=("parallel", "arbitrary"), vmem_limit_bytes=VMEM_LIMIT),
        name="norm_inproj",
    )(h2d, g.reshape(1, d), w_bf16)


def _hgrn_constants():
    c = HG_CHUNK
    t = np.arange(c)[:, None]
    s = np.arange(c)[None, :]
    tril = (s <= t).astype(np.float32)
    x = t ^ s
    top = np.where(x > 0, np.floor(np.log2(np.maximum(x, 1))).astype(np.int64), 0)
    half = 1 << top
    level = np.full((c, c), -1, np.int32)
    for li, hh in enumerate(HG_LEVELS):
        level = np.where((t > s) & (half == hh), li, level)
    level = np.where(t == s, HG_DIAG_LEVEL, level).astype(np.int32)
    return (jnp.asarray(tril, BF16), jnp.asarray(tril.T, BF16),
            jnp.asarray(level), jnp.asarray(level.T))


def _split3(x):
    h1 = x.astype(BF16)
    r1 = x - h1.astype(F32)
    h2 = r1.astype(BF16)
    r2 = r1 - h2.astype(F32)
    return h1, h2, r2.astype(BF16)


def _boundary_rows(cum_ref, half, rev):
    c = HG_CHUNK
    blk = 2 * half
    off = half if rev else half - 1
    if blk >= SUBLANES:
        parts = [jnp.broadcast_to(cum_ref[pl.ds(m * blk + off, 1), :], (blk, LANES))
                 for m in range(c // blk)]
        return jnp.concatenate(parts, axis=0)
    assert blk == 4
    sub = lax.broadcasted_iota(jnp.int32, (SUBLANES, LANES), 0)
    parts = []
    for g in range(c // SUBLANES):
        lo = jnp.broadcast_to(cum_ref[pl.ds(g * SUBLANES + off, 1), :], (SUBLANES, LANES))
        hi = jnp.broadcast_to(cum_ref[pl.ds(g * SUBLANES + blk + off, 1), :], (SUBLANES, LANES))
        parts.append(jnp.where(sub < blk, lo, hi))
    return jnp.concatenate(parts, axis=0)


def _hgrn_chunk(q, z, v, lb, st_ref, cum_ref, tri, level, rev):
    c = HG_CHUNK
    e = jnp.exp(-jnp.abs(z))
    r = 1.0 / (1.0 + e)
    pos = z >= 0.0
    sig = jnp.where(pos, r, e * r)
    nsig = jnp.where(pos, e * r, r)
    om = 1.0 - lb
    f = lb + om * sig
    kin = om * nsig
    lf = jnp.log(jnp.maximum(f, F_FLOOR))

    h1, h2, h3 = _split3(lf)
    cum = _dot(tri, h1) + _dot(tri, h2) + _dot(tri, h3)
    cum_ref[...] = cum
    tot = cum[0:1] if rev else cum[c - 1:c]

    q_dec = q * jnp.exp(cum)
    k_dec = kin * jnp.exp(tot - cum)
    st = st_ref[...]
    o = _dot_nt(q_dec.astype(BF16), st.astype(BF16))
    v_bf = v.astype(BF16)
    st_ref[...] = st * jnp.exp(tot) + _dot(v.T.astype(BF16), k_dec.astype(BF16))

    att = jnp.where(level == HG_DIAG_LEVEL, _dot_nt(q.astype(BF16), kin.astype(BF16)), 0.0)
    row = lax.broadcasted_iota(jnp.int32, (c, LANES), 0)
    for li, half in enumerate(HG_LEVELS):
        if half == 1:
            q_side = (row % 2 == 0) if rev else (row % 2 == 1)
            ex = jnp.where(q_side, jnp.exp(lf), 1.0)
        else:
            ex = jnp.exp(-jnp.abs(cum - _boundary_rows(cum_ref, half, rev)))
        a = _dot_nt((q * ex).astype(BF16), (kin * ex).astype(BF16))
        att = jnp.where(level == li, a, att)
    return o + _dot(att.astype(BF16), v_bf)


def _hgrn_kernel(q_ref, zf_ref, zb_ref, v_ref, gate_ref, lbl_ref, gain_ref,
                 tril_ref, triu_ref, lvf_ref, lvb_ref, o_ref,
                 of_s, ob_s, stf_s, stb_s, cumf_s, cumb_s, *, layer, seq_len):
    c = HG_CHUNK
    lp = q_ref.shape[1]
    nc = lp // c
    depth = lbl_ref.shape[0]

    logits = [lbl_ref[d] for d in range(depth)]
    mx = functools.reduce(jnp.maximum, logits)
    es = [jnp.exp(x - mx) for x in logits]
    den = functools.reduce(lambda a, b: a + b, es)
    num = jnp.zeros_like(den)
    for d in range(1, layer + 1):
        num = num + es[d]
    lbs = num / den
    lb_f = lbs[0:1]
    lb_b = lbs[1:2]

    stf_s[...] = jnp.zeros_like(stf_s)
    stb_s[...] = jnp.zeros_like(stb_s)

    def body(n, carry):
        rf = pl.ds(pl.multiple_of(n * c, c), c)
        rb = pl.ds(pl.multiple_of((nc - 1 - n) * c, c), c)
        of_s[rf, :] = _hgrn_chunk(
            q_ref[0, rf, :].astype(F32), zf_ref[0, rf, :].astype(F32), v_ref[0, rf, :].astype(F32),
            lb_f, stf_s, cumf_s, tril_ref[...], lvf_ref[...], False)
        ob_s[rb, :] = _hgrn_chunk(
            q_ref[0, rb, :].astype(F32), zb_ref[0, rb, :].astype(F32), v_ref[0, rb, :].astype(F32),
            lb_b, stb_s, cumb_s, triu_ref[...], lvb_ref[...], True)
        return carry

    lax.fori_loop(0, nc, body, 0)

    def epilogue(n, carry):
        rows = pl.ds(pl.multiple_of(n * c, c), c)
        o = of_s[rows, :] + ob_s[rows, :]
        ms = jnp.mean(o * o, axis=-1, keepdims=True)
        g = gate_ref[0, rows, :].astype(F32)
        y = o * lax.rsqrt(ms + NORM_EPS) * gain_ref[...] * (g * (1.0 / (1.0 + jnp.exp(-g))))
        pos = n * c + lax.broadcasted_iota(jnp.int32, (c, LANES), 0)
        o_ref[0, rows, :] = jnp.where(pos < seq_len, y, 0.0).astype(o_ref.dtype)
        return carry

    lax.fori_loop(0, nc, epilogue, 0)


def _hgrn(p3d, lb_logits, gain, layer, seq_len):
    b, lp, _ = p3d.shape
    c = HG_CHUNK
    depth = lb_logits.shape[0]
    tril, triu, lvf, lvb = _hgrn_constants()
    col = lambda g: pl.BlockSpec((1, lp, LANES), lambda i, h, g=g: (i, 0, g * HG_HEADS + h))
    const = lambda shape: pl.BlockSpec(shape, lambda i, h: (0,) * len(shape))
    return pl.pallas_call(
        functools.partial(_hgrn_kernel, layer=layer, seq_len=seq_len),
        out_shape=jax.ShapeDtypeStruct((b, lp, HG_HEADS * HG_VDIM), BF16),
        grid=(b, HG_HEADS),
        in_specs=[
            col(0), col(1), col(2), col(3), col(4),
            pl.BlockSpec((depth, 2, LANES), lambda i, h: (0, 0, h)),
            const((1, HG_VDIM)),
            const((c, c)), const((c, c)), const((c, c)), const((c, c)),
        ],
        out_specs=pl.BlockSpec((1, lp, LANES), lambda i, h: (i, 0, h)),
        scratch_shapes=[
            pltpu.VMEM((lp, HG_VDIM), F32), pltpu.VMEM((lp, HG_VDIM), F32),
            pltpu.VMEM((HG_VDIM, HG_KDIM), F32), pltpu.VMEM((HG_VDIM, HG_KDIM), F32),
            pltpu.VMEM((c, HG_KDIM), F32), pltpu.VMEM((c, HG_KDIM), F32),
        ],
        compiler_params=pltpu.CompilerParams(
            dimension_semantics=("parallel", "parallel"), vmem_limit_bytes=VMEM_LIMIT),
        name="hgrn2",
    )(p3d, p3d, p3d, p3d, p3d, lb_logits, gain.reshape(1, HG_VDIM), tril, triu, lvf, lvb)


def _rope_tables(lp):
    half = AT_HEAD_DIM // 2
    inv = ROPE_THETA ** (-jnp.arange(0, AT_HEAD_DIM, 2, dtype=F32) / AT_HEAD_DIM)
    ang = jnp.arange(lp, dtype=F32)[:, None] * inv[None, :]
    cos, sin = jnp.cos(ang), jnp.sin(ang)
    reps = LANES // half
    cos_t = jnp.tile(cos, (1, reps))
    sign = jnp.where((jnp.arange(LANES) % AT_HEAD_DIM) < half, -1.0, 1.0).astype(F32)
    sin_t = jnp.tile(sin, (1, reps)) * sign[None, :]
    return cos_t, sin_t


def _rope(t, cos_t, sin_t):
    half = AT_HEAD_DIM // 2
    lane = lax.broadcasted_iota(jnp.int32, t.shape, 1)
    first = (lane % AT_HEAD_DIM) < half
    partner = jnp.where(first, pltpu.roll(t, LANES - half, axis=1), pltpu.roll(t, half, axis=1))
    return t * cos_t + partner * sin_t


def _attn_kernel(q_ref, k_ref, v_ref, gate_ref, cosq_ref, sinq_ref, cosk_ref, sink_ref,
                 lamv_ref, gain_ref, o_ref, krot_s, *, lam_init, seq_len):
    tq = q_ref.shape[1]
    lp = k_ref.shape[1]
    qi = pl.program_id(2)

    @pl.when(qi == 0)
    def _():
        def rot(n, carry):
            rows = pl.ds(pl.multiple_of(n * LANES, LANES), LANES)
            krot_s[rows, :] = _rope(k_ref[0, rows, :].astype(F32),
                                    cosk_ref[rows, :], sink_ref[rows, :]).astype(BF16)
            return carry
        lax.fori_loop(0, lp // LANES, rot, 0)

    lv = lamv_ref[...]
    lam = (jnp.exp(jnp.sum(lv[0:1] * lv[1:2], axis=-1, keepdims=True))
           - jnp.exp(jnp.sum(lv[2:3] * lv[3:4], axis=-1, keepdims=True)) + lam_init)

    q = _rope(q_ref[0].astype(F32), cosq_ref[...], sinq_ref[...]) * (AT_HEAD_DIM ** -0.5)
    lane = lax.broadcasted_iota(jnp.int32, (tq, LANES), 1)
    key_pos = lax.broadcasted_iota(jnp.int32, (1, lp), 1)
    key_bias = jnp.where(key_pos < seq_len, 0.0, -1e30).astype(F32)
    k = krot_s[...]
    v = v_ref[0]

    outs = []
    for half_idx in range(2):
        in_half = (lane < AT_HEAD_DIM) if half_idx == 0 else (lane >= AT_HEAD_DIM)
        qc = jnp.where(in_half, q, 0.0).astype(BF16)
        s = _dot_nt(qc, k) + key_bias
        m = jnp.max(s, axis=-1, keepdims=True)
        p = jnp.exp(s - m)
        l = jnp.sum(p, axis=-1, keepdims=True)
        outs.append(_dot(p.astype(BF16), v) / l)
    o = outs[0] - lam * outs[1]

    ms = jnp.mean(o * o, axis=-1, keepdims=True)
    g = gate_ref[0].astype(F32)
    y = o * lax.rsqrt(ms + NORM_EPS) * gain_ref[...] * (1.0 - lam_init) * (g * (1.0 / (1.0 + jnp.exp(-g))))
    pos = qi * tq + lax.broadcasted_iota(jnp.int32, (tq, LANES), 0)
    o_ref[0] = jnp.where(pos < seq_len, y, 0.0).astype(o_ref.dtype)


def _attn(p3d, lam_vecs, gain, layer, seq_len, cos_t, sin_t):
    b, lp, _ = p3d.shape
    tq = _pick_tile(lp, 320, 16)
    lam_init = 0.8 - 0.6 * math.exp(-0.3 * layer)
    g0 = 5 * AT_HEADS
    qspec = lambda g: pl.BlockSpec((1, tq, LANES), lambda i, h, j, g=g: (i, j, g0 + g * AT_HEADS + h))
    kspec = lambda g: pl.BlockSpec((1, lp, LANES), lambda i, h, j, g=g: (i, 0, g0 + g * AT_HEADS + h))
    return pl.pallas_call(
        functools.partial(_attn_kernel, lam_init=lam_init, seq_len=seq_len),
        out_shape=jax.ShapeDtypeStruct((b, lp, AT_HEADS * 2 * AT_HEAD_DIM), BF16),
        grid=(b, AT_HEADS, lp // tq),
        in_specs=[
            qspec(0), kspec(1), kspec(2), qspec(3),
            pl.BlockSpec((tq, LANES), lambda i, h, j: (j, 0)),
            pl.BlockSpec((tq, LANES), lambda i, h, j: (j, 0)),
            pl.BlockSpec((lp, LANES), lambda i, h, j: (0, 0)),
            pl.BlockSpec((lp, LANES), lambda i, h, j: (0, 0)),
            pl.BlockSpec((4, AT_HEAD_DIM), lambda i, h, j: (0, 0)),
            pl.BlockSpec((1, LANES), lambda i, h, j: (0, 0)),
        ],
        out_specs=pl.BlockSpec((1, tq, LANES), lambda i, h, j: (i, j, h)),
        scratch_shapes=[pltpu.VMEM((lp, LANES), BF16)],
        compiler_params=pltpu.CompilerParams(
            dimension_semantics=("parallel", "parallel", "arbitrary"), vmem_limit_bytes=VMEM_LIMIT),
        name="diff_attn",
    )(p3d, p3d, p3d, p3d, cos_t, sin_t, cos_t, sin_t, lam_vecs, gain.reshape(1, LANES))


def _outproj_kernel(oh_ref, oa_ref, w_ref, h_ref, fg_ref, out_ref, *, final):
    wh = oh_ref.shape[1]
    acc = h_ref[...] + _dot(oh_ref[...], w_ref[0:wh, :]) + _dot(oa_ref[...], w_ref[wh:, :])
    if final:
        ms = jnp.mean(acc * acc, axis=-1, keepdims=True)
        acc = acc * lax.rsqrt(ms + NORM_EPS) * fg_ref[...]
    out_ref[...] = acc


def _outproj(oh2d, oa2d, w_bf16, h2d, final_g, final):
    m, d = h2d.shape
    wh, wa = oh2d.shape[1], oa2d.shape[1]
    tm = _pick_tile(m, 512, LANES)
    return pl.pallas_call(
        functools.partial(_outproj_kernel, final=final),
        out_shape=jax.ShapeDtypeStruct((m, d), F32),
        grid=(m // tm,),
        in_specs=[
            pl.BlockSpec((tm, wh), lambda i: (i, 0)),
            pl.BlockSpec((tm, wa), lambda i: (i, 0)),
            pl.BlockSpec((wh + wa, d), lambda i: (0, 0)),
            pl.BlockSpec((tm, d), lambda i: (i, 0)),
            pl.BlockSpec((1, d), lambda i: (0, 0)),
        ],
        out_specs=pl.BlockSpec((tm, d), lambda i: (i, 0)),
        compiler_params=pltpu.CompilerParams(
            dimension_semantics=("parallel",), vmem_limit_bytes=VMEM_LIMIT),
        name="outproj",
    )(oh2d, oa2d, w_bf16, h2d, final_g.reshape(1, d))


def kernel(x, meta_tokens, norm_g, w_in, hg_lb_logits, hg_norm_g, diff_lambda, diff_subln_g,
           w_out, final_norm_g):
    bsz, seq, d = x.shape
    depth = w_in.shape[0]
    seq_len = N_META + seq
    lp = -(-seq_len // LANES) * LANES
    assert w_in.shape[2] == N_GROUPS * HG_HEADS * LANES

    meta = jnp.broadcast_to(meta_tokens[None].astype(x.dtype), (bsz, N_META, d))
    pad = jnp.zeros((bsz, lp - seq_len, d), x.dtype)
    h = jnp.concatenate([meta, x, pad], axis=1).reshape(bsz * lp, d)
    cos_t, sin_t = _rope_tables(lp)

    for l in range(depth):
        p = _norm_inproj(h, norm_g[l], w_in[l].astype(BF16)).reshape(bsz, lp, -1)
        o_h = _hgrn(p, hg_lb_logits, hg_norm_g[l], l, seq_len)
        o_a = _attn(p, diff_lambda[l], diff_subln_g[l], l, seq_len, cos_t, sin_t)
        h = _outproj(o_h.reshape(bsz * lp, -1), o_a.reshape(bsz * lp, -1),
                     w_out[l].astype(BF16), h, final_norm_g, final=(l == depth - 1))
    return h.reshape(bsz, lp, d)[:, N_META:seq_len]
```

```python
import functools
import math

import numpy as np
import jax
import jax.numpy as jnp
from jax import lax
from jax.experimental import pallas as pl
from jax.experimental.pallas import tpu as pltpu

F32 = jnp.float32
BF16 = jnp.bfloat16

N_META = 16
HG_HEADS = 8
HG_KDIM = 128
HG_VDIM = 128
AT_HEADS = 8
AT_HEAD_DIM = 64
ROPE_THETA = 10000.0
NORM_EPS = 1e-6
F_FLOOR = 1e-30
N_GROUPS = 9
LOG2E = math.log2(math.e)

LANES = 128
SUBLANES = 8
VMEM_LIMIT = 56 * 1024 * 1024

HG_CHUNK = 128
HG_LEVELS = (64, 32, 16, 8, 4, 2, 1)
HG_DIAG_LEVEL = len(HG_LEVELS)
HG_HEADS_PER_STEP = 2

KEY_MASK = -1e30


def _pick_tile(n, cap, mult):
    best = None
    for t in range(mult, min(n, cap) + 1, mult):
        if n % t == 0:
            best = t
    assert best is not None, (n, cap, mult)
    return best


def _dot(a, b):
    return jnp.dot(a, b, preferred_element_type=F32)


def _dot_nt(a, b):
    return lax.dot_general(a, b, (((1,), (1,)), ((), ())), preferred_element_type=F32)


def _silu(g):
    return g * (1.0 / (1.0 + jnp.exp(-g)))


def _norm_inproj_kernel(h_ref, g_ref, w_ref, p_ref, u_ref):
    @pl.when(pl.program_id(1) == 0)
    def _():
        x = h_ref[...]
        ms = jnp.mean(x * x, axis=-1, keepdims=True)
        u_ref[...] = (x * lax.rsqrt(ms + NORM_EPS) * g_ref[...]).astype(BF16)

    p_ref[...] = _dot(u_ref[...], w_ref[...]).astype(BF16)


def _norm_inproj(h2d, g, w_bf16):
    m, d = h2d.shape
    n = w_bf16.shape[1]
    tm = _pick_tile(m, 1024, LANES)
    tn = _pick_tile(n, 1024, LANES)
    return pl.pallas_call(
        _norm_inproj_kernel,
        out_shape=jax.ShapeDtypeStruct((m, n), BF16),
        grid=(m // tm, n // tn),
        in_specs=[
            pl.BlockSpec((tm, d), lambda i, j: (i, 0)),
            pl.BlockSpec((1, d), lambda i, j: (0, 0)),
            pl.BlockSpec((d, tn), lambda i, j: (0, j)),
        ],
        out_specs=pl.BlockSpec((tm, tn), lambda i, j: (i, j)),
        scratch_shapes=[pltpu.VMEM((tm, d), BF16)],
        compiler_params=pltpu.CompilerParams(
            dimension_semantics=("parallel", "arbitrary"), vmem_limit_bytes=VMEM_LIMIT),
        name="norm_inproj",
    )(h2d, g.reshape(1, d), w_bf16)


def _hgrn_constants():
    c = HG_CHUNK
    t = np.arange(c)[:, None]
    s = np.arange(c)[None, :]
    tril = (s <= t).astype(np.float32)
    x = t ^ s
    top = np.where(x > 0, np.floor(np.log2(np.maximum(x, 1))).astype(np.int64), 0)
    half = 1 << top
    level = np.full((c, c), -1, np.int32)
    for li, hh in enumerate(HG_LEVELS):
        level = np.where((t > s) & (half == hh), li, level)
    level = np.where(t == s, HG_DIAG_LEVEL, level).astype(np.int32)
    return (jnp.asarray(tril, BF16), jnp.asarray(tril.T, BF16),
            jnp.asarray(level), jnp.asarray(level.T))


def _split3(x):
    h1 = x.astype(BF16)
    r1 = x - h1.astype(F32)
    h2 = r1.astype(BF16)
    r2 = r1 - h2.astype(F32)
    return h1, h2, r2.astype(BF16)


def _level_exponent(cum, cum_ref, half, rev):
    c = HG_CHUNK
    blk = 2 * half
    off = half if rev else half - 1
    parts = []
    if half >= SUBLANES:
        for m in range(c // blk):
            ref = cum_ref[pl.ds(m * blk + off, 1), :]
            first = cum[m * blk:m * blk + half]
            second = cum[m * blk + half:(m + 1) * blk]
            parts += [first - ref, ref - second] if rev else [ref - first, second - ref]
        return jnp.concatenate(parts, axis=0)
    sub = lax.broadcasted_iota(jnp.int32, (SUBLANES, LANES), 0)
    for g in range(c // SUBLANES):
        base = g * SUBLANES
        ref = jnp.broadcast_to(cum_ref[pl.ds(base + off, 1), :], (SUBLANES, LANES))
        if blk < SUBLANES:
            hi = jnp.broadcast_to(cum_ref[pl.ds(base + blk + off, 1), :], (SUBLANES, LANES))
            ref = jnp.where(sub < blk, ref, hi)
        parts.append(-jnp.abs(cum[base:base + SUBLANES] - ref))
    return jnp.concatenate(parts, axis=0)


def _hgrn_chunk(q, z, v, lb, st_ref, cum_ref, tri, level, rev):
    c = HG_CHUNK
    e = jnp.exp(-jnp.abs(z))
    r = 1.0 / (1.0 + e)
    pos = z >= 0.0
    sig = jnp.where(pos, r, e * r)
    nsig = jnp.where(pos, e * r, r)
    om = 1.0 - lb
    f = jnp.maximum(lb + om * sig, F_FLOOR)
    kin = om * nsig
    lf = jnp.log(f)

    h1, h2, h3 = _split3(lf)
    cum = (_dot(tri, h1) + _dot(tri, h2) + _dot(tri, h3)) * LOG2E
    cum_ref[...] = cum
    tot = cum[0:1] if rev else cum[c - 1:c]

    q_dec = q * jnp.exp2(cum)
    k_dec = kin * jnp.exp2(tot - cum)
    st = st_ref[...]
    o = _dot_nt(q_dec.astype(BF16), st.astype(BF16))
    v_bf = v.astype(BF16)
    st_ref[...] = st * jnp.exp2(tot) + _dot(v.T.astype(BF16), k_dec.astype(BF16))

    att = jnp.where(level == HG_DIAG_LEVEL, _dot_nt(q.astype(BF16), kin.astype(BF16)), 0.0)
    row = lax.broadcasted_iota(jnp.int32, (c, LANES), 0)
    for li, half in enumerate(HG_LEVELS):
        if half == 1:
            q_side = (row % 2 == 0) if rev else (row % 2 == 1)
            ex = jnp.where(q_side, f, 1.0)
        else:
            ex = jnp.exp2(_level_exponent(cum, cum_ref, half, rev))
        a = _dot_nt((q * ex).astype(BF16), (kin * ex).astype(BF16))
        att = jnp.where(level == li, a, att)
    return o + _dot(att.astype(BF16), v_bf)


def _hgrn_kernel(q_ref, zf_ref, zb_ref, v_ref, gate_ref, lbl_ref, gain_ref,
                 tril_ref, triu_ref, lvf_ref, lvb_ref, o_ref,
                 of_s, ob_s, stf_s, stb_s, cumf_s, cumb_s, *, layer, seq_len):
    c = HG_CHUNK
    lp = q_ref.shape[1]
    nc = lp // c
    depth = lbl_ref.shape[0]
    nheads = q_ref.shape[2] // LANES

    logits = [lbl_ref[d] for d in range(depth)]
    mx = functools.reduce(jnp.maximum, logits)
    es = [jnp.exp(x - mx) for x in logits]
    den = functools.reduce(lambda a, b: a + b, es)
    num = jnp.zeros_like(den)
    for d in range(1, layer + 1):
        num = num + es[d]
    lbs = num / den

    stf_s[...] = jnp.zeros_like(stf_s)
    stb_s[...] = jnp.zeros_like(stb_s)

    def body(n, carry):
        rf = pl.ds(pl.multiple_of(n * c, c), c)
        rb = pl.ds(pl.multiple_of((nc - 1 - n) * c, c), c)
        for hh in range(nheads):
            ln = pl.ds(hh * LANES, LANES)
            of_s[rf, ln] = _hgrn_chunk(
                q_ref[0, rf, ln].astype(F32), zf_ref[0, rf, ln].astype(F32), v_ref[0, rf, ln].astype(F32),
                lbs[0:1, hh * LANES:(hh + 1) * LANES], stf_s.at[hh], cumf_s.at[hh],
                tril_ref[...], lvf_ref[...], False)
            ob_s[rb, ln] = _hgrn_chunk(
                q_ref[0, rb, ln].astype(F32), zb_ref[0, rb, ln].astype(F32), v_ref[0, rb, ln].astype(F32),
                lbs[1:2, hh * LANES:(hh + 1) * LANES], stb_s.at[hh], cumb_s.at[hh],
                triu_ref[...], lvb_ref[...], True)
        return carry

    lax.fori_loop(0, nc, body, 0)

    def epilogue(n, carry):
        rows = pl.ds(pl.multiple_of(n * c, c), c)
        pos = n * c + lax.broadcasted_iota(jnp.int32, (c, LANES), 0)
        for hh in range(nheads):
            ln = pl.ds(hh * LANES, LANES)
            o = of_s[rows, ln] + ob_s[rows, ln]
            ms = jnp.mean(o * o, axis=-1, keepdims=True)
            y = o * lax.rsqrt(ms + NORM_EPS) * gain_ref[...] * _silu(gate_ref[0, rows, ln].astype(F32))
            o_ref[0, rows, ln] = jnp.where(pos < seq_len, y, 0.0).astype(o_ref.dtype)
        return carry

    lax.fori_loop(0, nc, epilogue, 0)


def _hgrn(p3d, lb_logits, gain, layer, seq_len):
    b, lp, _ = p3d.shape
    c = HG_CHUNK
    depth = lb_logits.shape[0]
    hps = HG_HEADS_PER_STEP
    wid = hps * LANES
    ngrp = HG_HEADS // hps
    tril, triu, lvf, lvb = _hgrn_constants()
    col = lambda g: pl.BlockSpec((1, lp, wid), lambda i, h, g=g: (i, 0, g * ngrp + h))
    const = lambda shape: pl.BlockSpec(shape, lambda i, h: (0,) * len(shape))
    return pl.pallas_call(
        functools.partial(_hgrn_kernel, layer=layer, seq_len=seq_len),
        out_shape=jax.ShapeDtypeStruct((b, lp, HG_HEADS * HG_VDIM), BF16),
        grid=(b, ngrp),
        in_specs=[
            col(0), col(1), col(2), col(3), col(4),
            pl.BlockSpec((depth, 2, wid), lambda i, h: (0, 0, h)),
            const((1, HG_VDIM)),
            const((c, c)), const((c, c)), const((c, c)), const((c, c)),
        ],
        out_specs=pl.BlockSpec((1, lp, wid), lambda i, h: (i, 0, h)),
        scratch_shapes=[
            pltpu.VMEM((lp, wid), F32), pltpu.VMEM((lp, wid), F32),
            pltpu.VMEM((hps, HG_VDIM, HG_KDIM), F32), pltpu.VMEM((hps, HG_VDIM, HG_KDIM), F32),
            pltpu.VMEM((hps, c, HG_KDIM), F32), pltpu.VMEM((hps, c, HG_KDIM), F32),
        ],
        compiler_params=pltpu.CompilerParams(
            dimension_semantics=("parallel", "parallel"), vmem_limit_bytes=VMEM_LIMIT),
        name="hgrn2",
    )(p3d, p3d, p3d, p3d, p3d, lb_logits, gain.reshape(1, HG_VDIM), tril, triu, lvf, lvb)


def _rope_tables(lp):
    half = AT_HEAD_DIM // 2
    inv = ROPE_THETA ** (-jnp.arange(0, AT_HEAD_DIM, 2, dtype=F32) / AT_HEAD_DIM)
    ang = jnp.arange(lp, dtype=F32)[:, None] * inv[None, :]
    cos, sin = jnp.cos(ang), jnp.sin(ang)
    reps = LANES // half
    cos_t = jnp.tile(cos, (1, reps))
    sign = jnp.where((jnp.arange(LANES) % AT_HEAD_DIM) < half, -1.0, 1.0).astype(F32)
    sin_t = jnp.tile(sin, (1, reps)) * sign[None, :]
    return cos_t, sin_t


def _rope(t, cos_t, sin_t):
    half = AT_HEAD_DIM // 2
    lane = lax.broadcasted_iota(jnp.int32, t.shape, 1)
    first = (lane % AT_HEAD_DIM) < half
    partner = jnp.where(first, pltpu.roll(t, LANES - half, axis=1), pltpu.roll(t, half, axis=1))
    return t * cos_t + partner * sin_t


def _attn_tile(lp):
    best = None
    for t in range(16, min(lp, 320) + 1, 16):
        if lp % t == 0 and (lp // t) % 2 == 0:
            best = t
    assert best is not None, lp
    return best


_BIAS_LANE = (AT_HEAD_DIM, 0)


def _attn_kernel(q_ref, k_ref, v_ref, gate_ref, cos_ref, sin_ref, lamv_ref, gain_ref, o_ref,
                 k_s, vaug_s, s_s, m_s, *, tq, lam_init, seq_len):
    lp = k_ref.shape[1]
    nq = lp // tq

    def prep(n, carry):
        rows = pl.ds(pl.multiple_of(n * LANES, LANES), LANES)
        kr = _rope(k_ref[0, rows, :].astype(F32), cos_ref[rows, :], sin_ref[rows, :])
        lane = lax.broadcasted_iota(jnp.int32, (LANES, LANES), 1)
        pos = n * LANES + lax.broadcasted_iota(jnp.int32, (LANES, LANES), 0)
        bias = jnp.where(pos < seq_len, 0.0, KEY_MASK)
        for c in range(2):
            in_half = (lane < AT_HEAD_DIM) if c == 0 else (lane >= AT_HEAD_DIM)
            kc = jnp.where(in_half, kr, jnp.where(lane == _BIAS_LANE[c], bias, 0.0))
            k_s[c, rows, :] = kc.astype(BF16)
        vaug_s[rows, 0:LANES] = v_ref[0, rows, :]
        vaug_s[rows, LANES:2 * LANES] = jnp.ones((LANES, LANES), BF16)
        return carry
    lax.fori_loop(0, lp // LANES, prep, 0)

    lv = lamv_ref[...]
    lam = (jnp.exp(jnp.sum(lv[0:1] * lv[1:2], axis=-1, keepdims=True))
           - jnp.exp(jnp.sum(lv[2:3] * lv[3:4], axis=-1, keepdims=True)) + lam_init)

    def scores(j, slot):
        rows = pl.ds(pl.multiple_of(j * tq, 16), tq)
        q = _rope(q_ref[0, rows, :].astype(F32), cos_ref[rows, :], sin_ref[rows, :])
        q = q * (AT_HEAD_DIM ** -0.5 * LOG2E)
        lane = lax.broadcasted_iota(jnp.int32, (tq, LANES), 1)
        for c in range(2):
            in_half = (lane < AT_HEAD_DIM) if c == 0 else (lane >= AT_HEAD_DIM)
            qc = jnp.where(in_half, q, jnp.where(lane == _BIAS_LANE[c], 1.0, 0.0)).astype(BF16)
            s = _dot_nt(qc, k_s[c])
            s_s[slot, c] = s
            m_s[slot, c] = jnp.max(s, axis=-1, keepdims=True)

    def outputs(j, slot):
        rows = pl.ds(pl.multiple_of(j * tq, 16), tq)
        outs = []
        for c in range(2):
            p = jnp.exp2(s_s[slot, c] - m_s[slot, c]).astype(BF16)
            ol = _dot(p, vaug_s[...])
            outs.append(ol[:, 0:LANES] / ol[:, LANES:2 * LANES])
        o = outs[0] - lam * outs[1]
        ms = jnp.mean(o * o, axis=-1, keepdims=True)
        y = (o * lax.rsqrt(ms + NORM_EPS) * gain_ref[...] * (1.0 - lam_init)
             * _silu(gate_ref[0, rows, :].astype(F32)))
        pos = j * tq + lax.broadcasted_iota(jnp.int32, (tq, LANES), 0)
        o_ref[0, rows, :] = jnp.where(pos < seq_len, y, 0.0).astype(o_ref.dtype)

    scores(0, 0)

    def pair(i, carry):
        j0 = 2 * i
        scores(j0 + 1, 1)
        outputs(j0, 0)
        scores(jnp.minimum(j0 + 2, nq - 1), 0)
        outputs(j0 + 1, 1)
        return carry
    lax.fori_loop(0, nq // 2, pair, 0)


def _attn(p3d, lam_vecs, gain, layer, seq_len, cos_t, sin_t):
    b, lp, _ = p3d.shape
    tq = _attn_tile(lp)
    lam_init = 0.8 - 0.6 * math.exp(-0.3 * layer)
    g0 = 5 * AT_HEADS
    col = lambda g: pl.BlockSpec((1, lp, LANES), lambda i, h, g=g: (i, 0, g0 + g * AT_HEADS + h))
    const = lambda shape: pl.BlockSpec(shape, lambda i, h: (0,) * len(shape))
    return pl.pallas_call(
        functools.partial(_attn_kernel, tq=tq, lam_init=lam_init, seq_len=seq_len),
        out_shape=jax.ShapeDtypeStruct((b, lp, AT_HEADS * 2 * AT_HEAD_DIM), BF16),
        grid=(b, AT_HEADS),
        in_specs=[
            col(0), col(1), col(2), col(3),
            const((lp, LANES)), const((lp, LANES)),
            const((4, AT_HEAD_DIM)), const((1, LANES)),
        ],
        out_specs=pl.BlockSpec((1, lp, LANES), lambda i, h: (i, 0, h)),
        scratch_shapes=[
            pltpu.VMEM((2, lp, LANES), BF16),
            pltpu.VMEM((lp, 2 * LANES), BF16),
            pltpu.VMEM((2, 2, tq, lp), F32),
            pltpu.VMEM((2, 2, tq, 1), F32),
        ],
        compiler_params=pltpu.CompilerParams(
            dimension_semantics=("parallel", "parallel"), vmem_limit_bytes=VMEM_LIMIT),
        name="diff_attn",
    )(p3d, p3d, p3d, p3d, cos_t, sin_t, lam_vecs, gain.reshape(1, LANES))


def _outproj_kernel(oh_ref, oa_ref, w_ref, h_ref, fg_ref, out_ref, *, final):
    wh = oh_ref.shape[1]
    acc = h_ref[...] + _dot(oh_ref[...], w_ref[0:wh, :]) + _dot(oa_ref[...], w_ref[wh:, :])
    if final:
        ms = jnp.mean(acc * acc, axis=-1, keepdims=True)
        acc = acc * lax.rsqrt(ms + NORM_EPS) * fg_ref[...]
    out_ref[...] = acc


def _outproj(oh2d, oa2d, w_bf16, h2d, final_g, final):
    m, d = h2d.shape
    wh, wa = oh2d.shape[1], oa2d.shape[1]
    tm = _pick_tile(m, 512, LANES)
    return pl.pallas_call(
        functools.partial(_outproj_kernel, final=final),
        out_shape=jax.ShapeDtypeStruct((m, d), F32),
        grid=(m // tm,),
        in_specs=[
            pl.BlockSpec((tm, wh), lambda i: (i, 0)),
            pl.BlockSpec((tm, wa), lambda i: (i, 0)),
            pl.BlockSpec((wh + wa, d), lambda i: (0, 0)),
            pl.BlockSpec((tm, d), lambda i: (i, 0)),
            pl.BlockSpec((1, d), lambda i: (0, 0)),
        ],
        out_specs=pl.BlockSpec((tm, d), lambda i: (i, 0)),
        compiler_params=pltpu.CompilerParams(
            dimension_semantics=("parallel",), vmem_limit_bytes=VMEM_LIMIT),
        name="outproj",
    )(oh2d, oa2d, w_bf16, h2d, final_g.reshape(1, d))


def kernel(x, meta_tokens, norm_g, w_in, hg_lb_logits, hg_norm_g, diff_lambda, diff_subln_g,
           w_out, final_norm_g):
    bsz, seq, d = x.shape
    depth = w_in.shape[0]
    seq_len = N_META + seq
    lp = -(-seq_len // LANES) * LANES
    assert w_in.shape[2] == N_GROUPS * HG_HEADS * LANES

    meta = jnp.broadcast_to(meta_tokens[None].astype(x.dtype), (bsz, N_META, d))
    pad = jnp.zeros((bsz, lp - seq_len, d), x.dtype)
    h = jnp.concatenate([meta, x, pad], axis=1).reshape(bsz * lp, d)
    cos_t, sin_t = _rope_tables(lp)

    for l in range(depth):
        p = _norm_inproj(h, norm_g[l], w_in[l].astype(BF16)).reshape(bsz, lp, -1)
        o_h = _hgrn(p, hg_lb_logits, hg_norm_g[l], l, seq_len)
        o_a = _attn(p, diff_lambda[l], diff_subln_g[l], l, seq_len, cos_t, sin_t)
        h = _outproj(o_h.reshape(bsz * lp, -1), o_a.reshape(bsz * lp, -1),
                     w_out[l].astype(BF16), h, final_norm_g, final=(l == depth - 1))
    return h.reshape(bsz, lp, d)[:, N_META:seq_len]
```

```python
import functools
import math

import numpy as np
import jax
import jax.numpy as jnp
from jax import lax
from jax.experimental import pallas as pl
from jax.experimental.pallas import tpu as pltpu

F32 = jnp.float32
BF16 = jnp.bfloat16

N_META = 16
HG_HEADS = 8
HG_KDIM = 128
HG_VDIM = 128
AT_HEADS = 8
AT_HEAD_DIM = 64
ROPE_THETA = 10000.0
NORM_EPS = 1e-6
F_FLOOR = 1e-30
N_GROUPS = 9
LOG2E = math.log2(math.e)

LANES = 128
SUBLANES = 8
VMEM_LIMIT = 56 * 1024 * 1024

HG_CHUNK = 128
HG_LEVELS = (64, 32, 16, 8, 4, 2, 1)
HG_DIAG_LEVEL = len(HG_LEVELS)
HG_HEADS_PER_STEP = 4

KEY_MASK = -1e30


def _pick_tile(n, cap, mult):
    best = None
    for t in range(mult, min(n, cap) + 1, mult):
        if n % t == 0:
            best = t
    assert best is not None, (n, cap, mult)
    return best


def _dot(a, b):
    return jnp.dot(a, b, preferred_element_type=F32)


def _dot_nt(a, b):
    return lax.dot_general(a, b, (((1,), (1,)), ((), ())), preferred_element_type=F32)


def _silu(g):
    return g * (1.0 / (1.0 + jnp.exp(-g)))


def _norm_inproj_kernel(h_ref, g_ref, w_ref, p_ref, u_ref):
    @pl.when(pl.program_id(1) == 0)
    def _():
        x = h_ref[...]
        ms = jnp.mean(x * x, axis=-1, keepdims=True)
        u_ref[...] = (x * lax.rsqrt(ms + NORM_EPS) * g_ref[...]).astype(BF16)

    p_ref[...] = _dot(u_ref[...], w_ref[...]).astype(BF16)


def _norm_inproj(h2d, g, w_bf16):
    m, d = h2d.shape
    n = w_bf16.shape[1]
    tm = _pick_tile(m, 1024, LANES)
    tn = _pick_tile(n, 1024, LANES)
    return pl.pallas_call(
        _norm_inproj_kernel,
        out_shape=jax.ShapeDtypeStruct((m, n), BF16),
        grid=(m // tm, n // tn),
        in_specs=[
            pl.BlockSpec((tm, d), lambda i, j: (i, 0)),
            pl.BlockSpec((1, d), lambda i, j: (0, 0)),
            pl.BlockSpec((d, tn), lambda i, j: (0, j)),
        ],
        out_specs=pl.BlockSpec((tm, tn), lambda i, j: (i, j)),
        scratch_shapes=[pltpu.VMEM((tm, d), BF16)],
        compiler_params=pltpu.CompilerParams(
            dimension_semantics=("parallel", "arbitrary"), vmem_limit_bytes=VMEM_LIMIT),
        name="norm_inproj",
    )(h2d, g.reshape(1, d), w_bf16)


def _hgrn_constants():
    c = HG_CHUNK
    t = np.arange(c)[:, None]
    s = np.arange(c)[None, :]
    tril = (s <= t).astype(np.float32)
    x = t ^ s
    top = np.where(x > 0, np.floor(np.log2(np.maximum(x, 1))).astype(np.int64), 0)
    half = 1 << top
    level = np.full((c, c), -1, np.int32)
    for li, hh in enumerate(HG_LEVELS):
        level = np.where((t > s) & (half == hh), li, level)
    level = np.where(t == s, HG_DIAG_LEVEL, level).astype(np.int32)
    return (jnp.asarray(tril, BF16), jnp.asarray(tril.T, BF16),
            jnp.asarray(level), jnp.asarray(level.T))


def _split3(x):
    h1 = x.astype(BF16)
    r1 = x - h1.astype(F32)
    h2 = r1.astype(BF16)
    r2 = r1 - h2.astype(F32)
    return h1, h2, r2.astype(BF16)


def _level_operands(q, kin, cum, cum_ref, half, rev):
    c = HG_CHUNK
    blk = 2 * half
    off = half if rev else half - 1
    zeros = jnp.zeros((half, LANES), F32)
    qt, kt = [], []
    for m in range(c // blk):
        ref = cum_ref[pl.ds(m * blk + off, 1), :]
        lo = slice(m * blk, m * blk + half)
        hi = slice(m * blk + half, (m + 1) * blk)
        if rev:
            qt += [q[lo] * jnp.exp2(cum[lo] - ref), zeros]
            kt += [zeros, kin[hi] * jnp.exp2(ref - cum[hi])]
        else:
            kt += [kin[lo] * jnp.exp2(ref - cum[lo]), zeros]
            qt += [zeros, q[hi] * jnp.exp2(cum[hi] - ref)]
    return jnp.concatenate(qt, axis=0), jnp.concatenate(kt, axis=0)


def _level_exponent(cum, cum_ref, half, rev):
    c = HG_CHUNK
    blk = 2 * half
    off = half if rev else half - 1
    parts = []
    assert half < SUBLANES
    sub = lax.broadcasted_iota(jnp.int32, (SUBLANES, LANES), 0)
    for g in range(c // SUBLANES):
        base = g * SUBLANES
        ref = jnp.broadcast_to(cum_ref[pl.ds(base + off, 1), :], (SUBLANES, LANES))
        if blk < SUBLANES:
            hi = jnp.broadcast_to(cum_ref[pl.ds(base + blk + off, 1), :], (SUBLANES, LANES))
            ref = jnp.where(sub < blk, ref, hi)
        parts.append(-jnp.abs(cum[base:base + SUBLANES] - ref))
    return jnp.concatenate(parts, axis=0)


def _hgrn_chunks(chains):
    c = HG_CHUNK
    row = lax.broadcasted_iota(jnp.int32, (c, LANES), 0)
    for ch in chains:
        sig = 1.0 / (1.0 + jnp.exp(-ch["z"]))
        f = ch["lb"] + (1.0 - ch["lb"]) * sig
        ch["kin"] = 1.0 - f
        ch["f"] = jnp.maximum(f, F_FLOOR)
        ch["parts"] = _split3(jnp.log(ch["f"]))
    for ch in chains:
        h1, h2, h3 = ch["parts"]
        cum = (_dot(ch["tri"], h1) + _dot(ch["tri"], h2) + _dot(ch["tri"], h3)) * LOG2E
        ch["cum_ref"][...] = cum
        ch["cum"] = cum
        ch["tot"] = cum[0:1] if ch["rev"] else cum[c - 1:c]
    for ch in chains:
        cum, tot = ch["cum"], ch["tot"]
        q_dec = ch["q"] * jnp.exp2(cum)
        k_dec = ch["kin"] * jnp.exp2(tot - cum)
        st = ch["st_ref"][...]
        ch["o"] = _dot_nt(q_dec.astype(BF16), st.astype(BF16))
        ch["v_bf"] = ch["v"].astype(BF16)
        ch["st_ref"][...] = st * jnp.exp2(tot) + _dot(ch["v"].T.astype(BF16), k_dec.astype(BF16))
        ch["att"] = jnp.where(ch["level"] == HG_DIAG_LEVEL,
                              _dot_nt(ch["q"].astype(BF16), ch["kin"].astype(BF16)), 0.0)
    for li, half in enumerate(HG_LEVELS):
        for ch in chains:
            if half >= SUBLANES:
                qt, kt = _level_operands(ch["q"], ch["kin"], ch["cum"], ch["cum_ref"], half, ch["rev"])
            else:
                if half == 1:
                    q_side = (row % 2 == 0) if ch["rev"] else (row % 2 == 1)
                    ex = jnp.where(q_side, ch["f"], 1.0)
                else:
                    ex = jnp.exp2(_level_exponent(ch["cum"], ch["cum_ref"], half, ch["rev"]))
                qt, kt = ch["q"] * ex, ch["kin"] * ex
            a = _dot_nt(qt.astype(BF16), kt.astype(BF16))
            ch["att"] = jnp.where(ch["level"] == li, a, ch["att"])
    return [ch["o"] + _dot(ch["att"].astype(BF16), ch["v_bf"]) for ch in chains]


def _hgrn_kernel(q_ref, zf_ref, zb_ref, v_ref, gate_ref, lbl_ref, gain_ref,
                 tril_ref, triu_ref, lvf_ref, lvb_ref, o_ref,
                 of_s, ob_s, stf_s, stb_s, cumf_s, cumb_s, *, layer, seq_len):
    c = HG_CHUNK
    lp = q_ref.shape[1]
    nc = lp // c
    depth = lbl_ref.shape[0]
    nheads = q_ref.shape[2] // LANES

    logits = [lbl_ref[d] for d in range(depth)]
    mx = functools.reduce(jnp.maximum, logits)
    es = [jnp.exp(x - mx) for x in logits]
    den = functools.reduce(lambda a, b: a + b, es)
    num = jnp.zeros_like(den)
    for d in range(1, layer + 1):
        num = num + es[d]
    lbs = num / den

    stf_s[...] = jnp.zeros_like(stf_s)
    stb_s[...] = jnp.zeros_like(stb_s)

    def body(n, carry):
        rf = pl.ds(pl.multiple_of(n * c, c), c)
        rb = pl.ds(pl.multiple_of((nc - 1 - n) * c, c), c)
        chains = []
        for hh in range(nheads):
            ln = pl.ds(hh * LANES, LANES)
            lanes = slice(hh * LANES, (hh + 1) * LANES)
            chains.append(dict(
                q=q_ref[0, rf, ln].astype(F32), z=zf_ref[0, rf, ln].astype(F32), v=v_ref[0, rf, ln].astype(F32),
                lb=lbs[0:1, lanes], st_ref=stf_s.at[hh], cum_ref=cumf_s.at[hh],
                tri=tril_ref[...], level=lvf_ref[...], rev=False, out=(of_s, rf, ln)))
            chains.append(dict(
                q=q_ref[0, rb, ln].astype(F32), z=zb_ref[0, rb, ln].astype(F32), v=v_ref[0, rb, ln].astype(F32),
                lb=lbs[1:2, lanes], st_ref=stb_s.at[hh], cum_ref=cumb_s.at[hh],
                tri=triu_ref[...], level=lvb_ref[...], rev=True, out=(ob_s, rb, ln)))
        for ch, o in zip(chains, _hgrn_chunks(chains)):
            dst, rows, ln = ch["out"]
            dst[rows, ln] = o
        return carry

    lax.fori_loop(0, nc, body, 0)

    def epilogue(n, carry):
        rows = pl.ds(pl.multiple_of(n * c, c), c)
        pos = n * c + lax.broadcasted_iota(jnp.int32, (c, LANES), 0)
        for hh in range(nheads):
            ln = pl.ds(hh * LANES, LANES)
            o = of_s[rows, ln] + ob_s[rows, ln]
            ms = jnp.mean(o * o, axis=-1, keepdims=True)
            y = o * lax.rsqrt(ms + NORM_EPS) * gain_ref[...] * _silu(gate_ref[0, rows, ln].astype(F32))
            o_ref[0, rows, ln] = jnp.where(pos < seq_len, y, 0.0).astype(o_ref.dtype)
        return carry

    lax.fori_loop(0, nc, epilogue, 0)


def _hgrn(p3d, lb_logits, gain, layer, seq_len):
    b, lp, _ = p3d.shape
    c = HG_CHUNK
    depth = lb_logits.shape[0]
    hps = HG_HEADS_PER_STEP
    wid = hps * LANES
    ngrp = HG_HEADS // hps
    tril, triu, lvf, lvb = _hgrn_constants()
    col = lambda g: pl.BlockSpec((1, lp, wid), lambda i, h, g=g: (i, 0, g * ngrp + h))
    const = lambda shape: pl.BlockSpec(shape, lambda i, h: (0,) * len(shape))
    return pl.pallas_call(
        functools.partial(_hgrn_kernel, layer=layer, seq_len=seq_len),
        out_shape=jax.ShapeDtypeStruct((b, lp, HG_HEADS * HG_VDIM), BF16),
        grid=(b, ngrp),
        in_specs=[
            col(0), col(1), col(2), col(3), col(4),
            pl.BlockSpec((depth, 2, wid), lambda i, h: (0, 0, h)),
            const((1, HG_VDIM)),
            const((c, c)), const((c, c)), const((c, c)), const((c, c)),
        ],
        out_specs=pl.BlockSpec((1, lp, wid), lambda i, h: (i, 0, h)),
        scratch_shapes=[
            pltpu.VMEM((lp, wid), F32), pltpu.VMEM((lp, wid), F32),
            pltpu.VMEM((hps, HG_VDIM, HG_KDIM), F32), pltpu.VMEM((hps, HG_VDIM, HG_KDIM), F32),
            pltpu.VMEM((hps, c, HG_KDIM), F32), pltpu.VMEM((hps, c, HG_KDIM), F32),
        ],
        compiler_params=pltpu.CompilerParams(
            dimension_semantics=("parallel", "parallel"), vmem_limit_bytes=VMEM_LIMIT),
        name="hgrn2",
    )(p3d, p3d, p3d, p3d, p3d, lb_logits, gain.reshape(1, HG_VDIM), tril, triu, lvf, lvb)


def _rope_tables(lp):
    half = AT_HEAD_DIM // 2
    inv = ROPE_THETA ** (-jnp.arange(0, AT_HEAD_DIM, 2, dtype=F32) / AT_HEAD_DIM)
    ang = jnp.arange(lp, dtype=F32)[:, None] * inv[None, :]
    cos, sin = jnp.cos(ang), jnp.sin(ang)
    reps = LANES // half
    cos_t = jnp.tile(cos, (1, reps))
    sign = jnp.where((jnp.arange(LANES) % AT_HEAD_DIM) < half, -1.0, 1.0).astype(F32)
    sin_t = jnp.tile(sin, (1, reps)) * sign[None, :]
    return cos_t, sin_t


def _rope(t, cos_t, sin_t):
    half = AT_HEAD_DIM // 2
    lane = lax.broadcasted_iota(jnp.int32, t.shape, 1)
    first = (lane % AT_HEAD_DIM) < half
    partner = jnp.where(first, pltpu.roll(t, LANES - half, axis=1), pltpu.roll(t, half, axis=1))
    return t * cos_t + partner * sin_t


AT_TQ_CAP = 704


def _attn_tile(seq_len):
    best = None
    for t in range(16, min(seq_len, AT_TQ_CAP) + 1, 16):
        if seq_len % t == 0:
            best = t
    assert best is not None, seq_len
    return best


_BIAS_LANE = (AT_HEAD_DIM, 0)


def _attn_kernel(q_ref, k_ref, v_ref, gate_ref, cos_ref, sin_ref, lamv_ref, gain_ref, o_ref,
                 k_s, vaug_s, s_s, m_s, *, tq, lam_init, seq_len):
    lp = k_ref.shape[1]
    nq = seq_len // tq

    tk = _pick_tile(lp, AT_TQ_CAP, LANES)
    for n in range(lp // tk):
        rows = pl.ds(n * tk, tk)
        kr = _rope(k_ref[0, rows, :].astype(F32), cos_ref[rows, :], sin_ref[rows, :])
        lane = lax.broadcasted_iota(jnp.int32, (tk, LANES), 1)
        pos = n * tk + lax.broadcasted_iota(jnp.int32, (tk, LANES), 0)
        bias = jnp.where(pos < seq_len, 0.0, KEY_MASK)
        for c in range(2):
            in_half = (lane < AT_HEAD_DIM) if c == 0 else (lane >= AT_HEAD_DIM)
            kc = jnp.where(in_half, kr, jnp.where(lane == _BIAS_LANE[c], bias, 0.0))
            k_s[c, rows, :] = kc.astype(BF16)
        vaug_s[rows, 0:LANES] = v_ref[0, rows, :]
        vaug_s[rows, LANES:2 * LANES] = jnp.ones((tk, LANES), BF16)

    lv = lamv_ref[...]
    lam = (jnp.exp(jnp.sum(lv[0:1] * lv[1:2], axis=-1, keepdims=True))
           - jnp.exp(jnp.sum(lv[2:3] * lv[3:4], axis=-1, keepdims=True)) + lam_init)

    def scores(j, slot):
        rows = pl.ds(j * tq, tq)
        q = _rope(q_ref[0, rows, :].astype(F32), cos_ref[rows, :], sin_ref[rows, :])
        q = q * (AT_HEAD_DIM ** -0.5 * LOG2E)
        lane = lax.broadcasted_iota(jnp.int32, (tq, LANES), 1)
        for c in range(2):
            in_half = (lane < AT_HEAD_DIM) if c == 0 else (lane >= AT_HEAD_DIM)
            qc = jnp.where(in_half, q, jnp.where(lane == _BIAS_LANE[c], 1.0, 0.0)).astype(BF16)
            s = _dot_nt(qc, k_s[c])
            s_s[slot, c] = s
            m_s[slot, c] = jnp.max(s, axis=-1, keepdims=True)

    def outputs(j, slot):
        rows = pl.ds(j * tq, tq)
        outs = []
        for c in range(2):
            p = jnp.exp2(s_s[slot, c] - m_s[slot, c]).astype(BF16)
            ol = _dot(p, vaug_s[...])
            outs.append(ol[:, 0:LANES] / ol[:, LANES:2 * LANES])
        o = outs[0] - lam * outs[1]
        ms = jnp.mean(o * o, axis=-1, keepdims=True)
        y = (o * lax.rsqrt(ms + NORM_EPS) * gain_ref[...] * (1.0 - lam_init)
             * _silu(gate_ref[0, rows, :].astype(F32)))
        o_ref[0, rows, :] = y.astype(o_ref.dtype)

    scores(0, 0)
    for j in range(nq):
        if j + 1 < nq:
            scores(j + 1, (j + 1) % 2)
        outputs(j, j % 2)
    if seq_len < lp:
        o_ref[0, seq_len:lp, :] = jnp.zeros((lp - seq_len, LANES), o_ref.dtype)


def _attn(p3d, lam_vecs, gain, layer, seq_len, cos_t, sin_t):
    b, lp, _ = p3d.shape
    tq = _attn_tile(seq_len)
    lam_init = 0.8 - 0.6 * math.exp(-0.3 * layer)
    g0 = 5 * AT_HEADS
    col = lambda g: pl.BlockSpec((1, lp, LANES), lambda i, h, g=g: (i, 0, g0 + g * AT_HEADS + h))
    const = lambda shape: pl.BlockSpec(shape, lambda i, h: (0,) * len(shape))
    return pl.pallas_call(
        functools.partial(_attn_kernel, tq=tq, lam_init=lam_init, seq_len=seq_len),
        out_shape=jax.ShapeDtypeStruct((b, lp, AT_HEADS * 2 * AT_HEAD_DIM), BF16),
        grid=(b, AT_HEADS),
        in_specs=[
            col(0), col(1), col(2), col(3),
            const((lp, LANES)), const((lp, LANES)),
            const((4, AT_HEAD_DIM)), const((1, LANES)),
        ],
        out_specs=pl.BlockSpec((1, lp, LANES), lambda i, h: (i, 0, h)),
        scratch_shapes=[
            pltpu.VMEM((2, lp, LANES), BF16),
            pltpu.VMEM((lp, 2 * LANES), BF16),
            pltpu.VMEM((2, 2, tq, lp), F32),
            pltpu.VMEM((2, 2, tq, 1), F32),
        ],
        compiler_params=pltpu.CompilerParams(
            dimension_semantics=("parallel", "parallel"), vmem_limit_bytes=VMEM_LIMIT),
        name="diff_attn",
    )(p3d, p3d, p3d, p3d, cos_t, sin_t, lam_vecs, gain.reshape(1, LANES))


def _outproj_kernel(oh_ref, oa_ref, w_ref, h_ref, fg_ref, out_ref, *, final):
    wh = oh_ref.shape[1]
    acc = h_ref[...] + _dot(oh_ref[...], w_ref[0:wh, :]) + _dot(oa_ref[...], w_ref[wh:, :])
    if final:
        ms = jnp.mean(acc * acc, axis=-1, keepdims=True)
        acc = acc * lax.rsqrt(ms + NORM_EPS) * fg_ref[...]
    out_ref[...] = acc


def _outproj(oh2d, oa2d, w_bf16, h2d, final_g, final):
    m, d = h2d.shape
    wh, wa = oh2d.shape[1], oa2d.shape[1]
    tm = _pick_tile(m, 512, LANES)
    return pl.pallas_call(
        functools.partial(_outproj_kernel, final=final),
        out_shape=jax.ShapeDtypeStruct((m, d), F32),
        grid=(m // tm,),
        in_specs=[
            pl.BlockSpec((tm, wh), lambda i: (i, 0)),
            pl.BlockSpec((tm, wa), lambda i: (i, 0)),
            pl.BlockSpec((wh + wa, d), lambda i: (0, 0)),
            pl.BlockSpec((tm, d), lambda i: (i, 0)),
            pl.BlockSpec((1, d), lambda i: (0, 0)),
        ],
        out_specs=pl.BlockSpec((tm, d), lambda i: (i, 0)),
        compiler_params=pltpu.CompilerParams(
            dimension_semantics=("parallel",), vmem_limit_bytes=VMEM_LIMIT),
        name="outproj",
    )(oh2d, oa2d, w_bf16, h2d, final_g.reshape(1, d))


def kernel(x, meta_tokens, norm_g, w_in, hg_lb_logits, hg_norm_g, diff_lambda, diff_subln_g,
           w_out, final_norm_g):
    bsz, seq, d = x.shape
    depth = w_in.shape[0]
    seq_len = N_META + seq
    lp = -(-seq_len // LANES) * LANES
    assert w_in.shape[2] == N_GROUPS * HG_HEADS * LANES

    meta = jnp.broadcast_to(meta_tokens[None].astype(x.dtype), (bsz, N_META, d))
    pad = jnp.zeros((bsz, lp - seq_len, d), x.dtype)
    h = jnp.concatenate([meta, x, pad], axis=1).reshape(bsz * lp, d)
    cos_t, sin_t = _rope_tables(lp)

    for l in range(depth):
        p = _norm_inproj(h, norm_g[l], w_in[l].astype(BF16)).reshape(bsz, lp, -1)
        o_h = _hgrn(p, hg_lb_logits, hg_norm_g[l], l, seq_len)
        o_a = _attn(p, diff_lambda[l], diff_subln_g[l], l, seq_len, cos_t, sin_t)
        h = _outproj(o_h.reshape(bsz * lp, -1), o_a.reshape(bsz * lp, -1),
                     w_out[l].astype(BF16), h, final_norm_g, final=(l == depth - 1))
    return h.reshape(bsz, lp, d)[:, N_META:seq_len]
```

```python
import functools
import math

import numpy as np
import jax
import jax.numpy as jnp
from jax import lax
from jax.experimental import pallas as pl
from jax.experimental.pallas import tpu as pltpu

F32 = jnp.float32
BF16 = jnp.bfloat16

N_META = 16
HG_HEADS = 8
HG_KDIM = 128
HG_VDIM = 128
AT_HEADS = 8
AT_HEAD_DIM = 64
ROPE_THETA = 10000.0
NORM_EPS = 1e-6
F_FLOOR = 1e-30
N_GROUPS = 9
LOG2E = math.log2(math.e)

LANES = 128
SUBLANES = 8
VMEM_LIMIT = 56 * 1024 * 1024

HG_CHUNK = 128
HG_LEVELS = (64, 32, 16, 8, 4, 2, 1)
HG_DIAG_LEVEL = len(HG_LEVELS)
HG_HEADS_PER_STEP = 4

KEY_MASK = -1e30


def _pick_tile(n, cap, mult):
    best = None
    for t in range(mult, min(n, cap) + 1, mult):
        if n % t == 0:
            best = t
    assert best is not None, (n, cap, mult)
    return best


def _dot(a, b):
    return jnp.dot(a, b, preferred_element_type=F32)


def _dot_nt(a, b):
    return lax.dot_general(a, b, (((1,), (1,)), ((), ())), preferred_element_type=F32)


def _silu(g):
    return g * (1.0 / (1.0 + jnp.exp(-g)))


def _norm_inproj_kernel(h_ref, g_ref, w_ref, p_ref, u_ref):
    @pl.when(pl.program_id(1) == 0)
    def _():
        x = h_ref[...]
        ms = jnp.mean(x * x, axis=-1, keepdims=True)
        u_ref[...] = (x * lax.rsqrt(ms + NORM_EPS) * g_ref[...]).astype(BF16)

    p_ref[...] = _dot(u_ref[...], w_ref[...]).astype(BF16)


def _norm_inproj(h2d, g, w_bf16, layer):
    m, d = h2d.shape
    n = w_bf16.shape[2]
    tm = _pick_tile(m, 2304, LANES)
    tn = _pick_tile(n, 1024, LANES)
    return pl.pallas_call(
        _norm_inproj_kernel,
        out_shape=jax.ShapeDtypeStruct((m, n), BF16),
        grid=(m // tm, n // tn),
        in_specs=[
            pl.BlockSpec((tm, d), lambda i, j: (i, 0)),
            pl.BlockSpec((1, d), lambda i, j: (0, 0)),
            pl.BlockSpec((None, d, tn), lambda i, j: (layer, 0, j)),
        ],
        out_specs=pl.BlockSpec((tm, tn), lambda i, j: (i, j)),
        scratch_shapes=[pltpu.VMEM((tm, d), BF16)],
        compiler_params=pltpu.CompilerParams(
            dimension_semantics=("parallel", "arbitrary"), vmem_limit_bytes=VMEM_LIMIT),
        name="norm_inproj",
    )(h2d, g.reshape(1, d), w_bf16)


def _hgrn_constants():
    c = HG_CHUNK
    t = np.arange(c)[:, None]
    s = np.arange(c)[None, :]
    tril = (s <= t).astype(np.float32)
    x = t ^ s
    top = np.where(x > 0, np.floor(np.log2(np.maximum(x, 1))).astype(np.int64), 0)
    half = 1 << top
    level = np.full((c, c), -1, np.int32)
    for li, hh in enumerate(HG_LEVELS):
        level = np.where((t > s) & (half == hh), li, level)
    level = np.where(t == s, HG_DIAG_LEVEL, level).astype(np.int32)
    return (jnp.asarray(tril, BF16), jnp.asarray(tril.T, BF16),
            jnp.asarray(level), jnp.asarray(level.T))


def _split3(x):
    h1 = x.astype(BF16)
    r1 = x - h1.astype(F32)
    h2 = r1.astype(BF16)
    r2 = r1 - h2.astype(F32)
    return h1, h2, r2.astype(BF16)


def _level_operands(q, kin, cum, cum_ref, half, rev):
    c = HG_CHUNK
    blk = 2 * half
    off = half if rev else half - 1
    zeros = jnp.zeros((half, LANES), F32)
    qt, kt = [], []
    for m in range(c // blk):
        ref = cum_ref[pl.ds(m * blk + off, 1), :]
        lo = slice(m * blk, m * blk + half)
        hi = slice(m * blk + half, (m + 1) * blk)
        if rev:
            qt += [q[lo] * jnp.exp2(cum[lo] - ref), zeros]
            kt += [zeros, kin[hi] * jnp.exp2(ref - cum[hi])]
        else:
            kt += [kin[lo] * jnp.exp2(ref - cum[lo]), zeros]
            qt += [zeros, q[hi] * jnp.exp2(cum[hi] - ref)]
    return jnp.concatenate(qt, axis=0), jnp.concatenate(kt, axis=0)


def _level_exponent(cum, cum_ref, half, rev):
    c = HG_CHUNK
    blk = 2 * half
    off = half if rev else half - 1
    parts = []
    assert half < SUBLANES
    sub = lax.broadcasted_iota(jnp.int32, (SUBLANES, LANES), 0)
    for g in range(c // SUBLANES):
        base = g * SUBLANES
        ref = jnp.broadcast_to(cum_ref[pl.ds(base + off, 1), :], (SUBLANES, LANES))
        if blk < SUBLANES:
            hi = jnp.broadcast_to(cum_ref[pl.ds(base + blk + off, 1), :], (SUBLANES, LANES))
            ref = jnp.where(sub < blk, ref, hi)
        parts.append(-jnp.abs(cum[base:base + SUBLANES] - ref))
    return jnp.concatenate(parts, axis=0)


def _hgrn_chunks(chains):
    c = HG_CHUNK
    row = lax.broadcasted_iota(jnp.int32, (c, LANES), 0)
    for ch in chains:
        sig = 1.0 / (1.0 + jnp.exp(-ch["z"]))
        f = ch["lb"] + (1.0 - ch["lb"]) * sig
        ch["kin"] = 1.0 - f
        ch["f"] = jnp.where(ch["valid"], jnp.maximum(f, F_FLOOR), 1.0)
        ch["parts"] = _split3(jnp.log(ch["f"]))
    for ch in chains:
        h1, h2, h3 = ch["parts"]
        cum = (_dot(ch["tri"], h1) + _dot(ch["tri"], h2) + _dot(ch["tri"], h3)) * LOG2E
        ch["cum_ref"][...] = cum
        ch["cum"] = cum
        ch["tot"] = cum[0:1] if ch["rev"] else cum[c - 1:c]
    for ch in chains:
        cum, tot = ch["cum"], ch["tot"]
        q_dec = ch["q"] * jnp.exp2(cum)
        k_dec = ch["kin"] * jnp.exp2(tot - cum)
        st = ch["st_ref"][...]
        ch["o"] = _dot_nt(q_dec.astype(BF16), st.astype(BF16))
        ch["v_bf"] = ch["v"].astype(BF16)
        ch["st_ref"][...] = st * jnp.exp2(tot) + _dot(ch["v"].T.astype(BF16), k_dec.astype(BF16))
        ch["att"] = jnp.where(ch["level"] == HG_DIAG_LEVEL,
                              _dot_nt(ch["q"].astype(BF16), ch["kin"].astype(BF16)), 0.0)
    for li, half in enumerate(HG_LEVELS):
        for ch in chains:
            if half >= SUBLANES:
                qt, kt = _level_operands(ch["q"], ch["kin"], ch["cum"], ch["cum_ref"], half, ch["rev"])
            else:
                if half == 1:
                    q_side = (row % 2 == 0) if ch["rev"] else (row % 2 == 1)
                    ex = jnp.where(q_side, ch["f"], 1.0)
                else:
                    ex = jnp.exp2(_level_exponent(ch["cum"], ch["cum_ref"], half, ch["rev"]))
                qt, kt = ch["q"] * ex, ch["kin"] * ex
            a = _dot_nt(qt.astype(BF16), kt.astype(BF16))
            ch["att"] = jnp.where(ch["level"] == li, a, ch["att"])
    return [ch["o"] + _dot(ch["att"].astype(BF16), ch["v_bf"]) for ch in chains]


def _hgrn_kernel(q_ref, zf_ref, zb_ref, v_ref, gate_ref, lbl_ref, gain_ref,
                 tril_ref, triu_ref, lvf_ref, lvb_ref, o_ref,
                 of_s, ob_s, stf_s, stb_s, cumf_s, cumb_s, *, layer, seq_len):
    c = HG_CHUNK
    lp = q_ref.shape[1]
    nc = lp // c
    depth = lbl_ref.shape[0]
    nheads = q_ref.shape[2] // LANES

    logits = [lbl_ref[d] for d in range(depth)]
    mx = functools.reduce(jnp.maximum, logits)
    es = [jnp.exp(x - mx) for x in logits]
    den = functools.reduce(lambda a, b: a + b, es)
    num = jnp.zeros_like(den)
    for d in range(1, layer + 1):
        num = num + es[d]
    lbs = num / den

    stf_s[...] = jnp.zeros_like(stf_s)
    stb_s[...] = jnp.zeros_like(stb_s)

    row = lax.broadcasted_iota(jnp.int32, (c, LANES), 0)

    def body(n, carry):
        base_f = pl.multiple_of(lax.rem(n + nc - 1, nc) * c, c)
        base_b = pl.multiple_of(lax.rem(2 * nc - 2 - n, nc) * c, c)
        rf = pl.ds(base_f, c)
        rb = pl.ds(base_b, c)
        valid_f = base_f + row < seq_len
        valid_b = base_b + row < seq_len
        chains = []
        for hh in range(nheads):
            ln = pl.ds(hh * LANES, LANES)
            lanes = slice(hh * LANES, (hh + 1) * LANES)
            chains.append(dict(
                q=q_ref[0, rf, ln].astype(F32), z=zf_ref[0, rf, ln].astype(F32), v=v_ref[0, rf, ln].astype(F32),
                lb=lbs[0:1, lanes], valid=valid_f, st_ref=stf_s.at[hh], cum_ref=cumf_s.at[hh],
                tri=tril_ref[...], level=lvf_ref[...], rev=False, out=(of_s, rf, ln)))
            chains.append(dict(
                q=q_ref[0, rb, ln].astype(F32), z=zb_ref[0, rb, ln].astype(F32), v=v_ref[0, rb, ln].astype(F32),
                lb=lbs[1:2, lanes], valid=valid_b, st_ref=stb_s.at[hh], cum_ref=cumb_s.at[hh],
                tri=triu_ref[...], level=lvb_ref[...], rev=True, out=(ob_s, rb, ln)))
        for ch, o in zip(chains, _hgrn_chunks(chains)):
            dst, rows, ln = ch["out"]
            dst[rows, ln] = o
        return carry

    lax.fori_loop(0, nc, body, 0)

    def epilogue(n, carry):
        rows = pl.ds(pl.multiple_of(n * c, c), c)
        pos = n * c + lax.broadcasted_iota(jnp.int32, (c, LANES), 0)
        for hh in range(nheads):
            ln = pl.ds(hh * LANES, LANES)
            o = of_s[rows, ln] + ob_s[rows, ln]
            ms = jnp.mean(o * o, axis=-1, keepdims=True)
            y = o * lax.rsqrt(ms + NORM_EPS) * gain_ref[...] * _silu(gate_ref[0, rows, ln].astype(F32))
            o_ref[0, rows, ln] = jnp.where(pos < seq_len, y, 0.0).astype(o_ref.dtype)
        return carry

    lax.fori_loop(0, nc, epilogue, 0)


def _hgrn(p3d, lb_logits, gain, layer, seq_len):
    b, lp, _ = p3d.shape
    c = HG_CHUNK
    depth = lb_logits.shape[0]
    hps = HG_HEADS_PER_STEP
    wid = hps * LANES
    ngrp = HG_HEADS // hps
    tril, triu, lvf, lvb = _hgrn_constants()
    col = lambda g: pl.BlockSpec((1, lp, wid), lambda i, h, g=g: (i, 0, g * ngrp + h))
    const = lambda shape: pl.BlockSpec(shape, lambda i, h: (0,) * len(shape))
    return pl.pallas_call(
        functools.partial(_hgrn_kernel, layer=layer, seq_len=seq_len),
        out_shape=jax.ShapeDtypeStruct((b, lp, HG_HEADS * HG_VDIM), BF16),
        grid=(b, ngrp),
        in_specs=[
            col(0), col(1), col(2), col(3), col(4),
            pl.BlockSpec((depth, 2, wid), lambda i, h: (0, 0, h)),
            const((1, HG_VDIM)),
            const((c, c)), const((c, c)), const((c, c)), const((c, c)),
        ],
        out_specs=pl.BlockSpec((1, lp, wid), lambda i, h: (i, 0, h)),
        scratch_shapes=[
            pltpu.VMEM((lp, wid), F32), pltpu.VMEM((lp, wid), F32),
            pltpu.VMEM((hps, HG_VDIM, HG_KDIM), F32), pltpu.VMEM((hps, HG_VDIM, HG_KDIM), F32),
            pltpu.VMEM((hps, c, HG_KDIM), F32), pltpu.VMEM((hps, c, HG_KDIM), F32),
        ],
        compiler_params=pltpu.CompilerParams(
            dimension_semantics=("parallel", "parallel"), vmem_limit_bytes=VMEM_LIMIT),
        name="hgrn2",
    )(p3d, p3d, p3d, p3d, p3d, lb_logits, gain.reshape(1, HG_VDIM), tril, triu, lvf, lvb)


def _rope_tables(positions):
    half = AT_HEAD_DIM // 2
    inv = ROPE_THETA ** (-jnp.arange(0, AT_HEAD_DIM, 2, dtype=F32) / AT_HEAD_DIM)
    ang = positions.astype(F32)[:, None] * inv[None, :]
    cos, sin = jnp.cos(ang), jnp.sin(ang)
    reps = LANES // half
    cos_t = jnp.tile(cos, (1, reps))
    sign = jnp.where((jnp.arange(LANES) % AT_HEAD_DIM) < half, -1.0, 1.0).astype(F32)
    sin_t = jnp.tile(sin, (1, reps)) * sign[None, :]
    return cos_t, sin_t


def _rope(t, cos_t, sin_t):
    half = AT_HEAD_DIM // 2
    lane = lax.broadcasted_iota(jnp.int32, t.shape, 1)
    first = (lane % AT_HEAD_DIM) < half
    partner = jnp.where(first, pltpu.roll(t, LANES - half, axis=1), pltpu.roll(t, half, axis=1))
    return t * cos_t + partner * sin_t


AT_TQ_CAP = 704


def _attn_tile(seq_len):
    best = None
    for t in range(16, min(seq_len, AT_TQ_CAP) + 1, 16):
        if seq_len % t == 0:
            best = t
    assert best is not None, seq_len
    return best


_BIAS_LANE = (AT_HEAD_DIM, 0)


def _attn_kernel(q_ref, k_ref, v_ref, gate_ref, cos_ref, sin_ref, lamv_ref, gain_ref, o_ref,
                 k_s, vaug_s, s_s, m_s, *, tq, lam_init, seq_len):
    lp = k_ref.shape[1]
    nq = seq_len // tq

    tk = _pick_tile(lp, AT_TQ_CAP, LANES)
    for n in range(lp // tk):
        rows = pl.ds(n * tk, tk)
        kr = _rope(k_ref[0, rows, :].astype(F32), cos_ref[rows, :], sin_ref[rows, :])
        lane = lax.broadcasted_iota(jnp.int32, (tk, LANES), 1)
        pos = n * tk + lax.broadcasted_iota(jnp.int32, (tk, LANES), 0)
        bias = jnp.where(pos < seq_len, 0.0, KEY_MASK)
        for c in range(2):
            in_half = (lane < AT_HEAD_DIM) if c == 0 else (lane >= AT_HEAD_DIM)
            kc = jnp.where(in_half, kr, jnp.where(lane == _BIAS_LANE[c], bias, 0.0))
            k_s[c, rows, :] = kc.astype(BF16)
        vaug_s[rows, 0:LANES] = v_ref[0, rows, :]
        vaug_s[rows, LANES:2 * LANES] = jnp.ones((tk, LANES), BF16)

    lv = lamv_ref[...]
    lam = (jnp.exp(jnp.sum(lv[0:1] * lv[1:2], axis=-1, keepdims=True))
           - jnp.exp(jnp.sum(lv[2:3] * lv[3:4], axis=-1, keepdims=True)) + lam_init)

    def scores(j, slot):
        rows = pl.ds(j * tq, tq)
        q = _rope(q_ref[0, rows, :].astype(F32), cos_ref[rows, :], sin_ref[rows, :])
        q = q * (AT_HEAD_DIM ** -0.5 * LOG2E)
        lane = lax.broadcasted_iota(jnp.int32, (tq, LANES), 1)
        for c in range(2):
            in_half = (lane < AT_HEAD_DIM) if c == 0 else (lane >= AT_HEAD_DIM)
            qc = jnp.where(in_half, q, jnp.where(lane == _BIAS_LANE[c], 1.0, 0.0)).astype(BF16)
            s = _dot_nt(qc, k_s[c])
            s_s[slot, c] = s
            m_s[slot, c] = jnp.max(s, axis=-1, keepdims=True)

    def outputs(j, slot):
        rows = pl.ds(j * tq, tq)
        outs = []
        for c in range(2):
            p = jnp.exp2(s_s[slot, c] - m_s[slot, c]).astype(BF16)
            ol = _dot(p, vaug_s[...])
            outs.append(ol[:, 0:LANES] / ol[:, LANES:2 * LANES])
        o = outs[0] - lam * outs[1]
        ms = jnp.mean(o * o, axis=-1, keepdims=True)
        y = (o * lax.rsqrt(ms + NORM_EPS) * gain_ref[...] * (1.0 - lam_init)
             * _silu(gate_ref[0, rows, :].astype(F32)))
        o_ref[0, rows, :] = y.astype(o_ref.dtype)

    scores(0, 0)
    for j in range(nq):
        if j + 1 < nq:
            scores(j + 1, (j + 1) % 2)
        outputs(j, j % 2)
    if seq_len < lp:
        o_ref[0, seq_len:lp, :] = jnp.zeros((lp - seq_len, LANES), o_ref.dtype)


def _attn(p3d, lam_vecs, gain, layer, seq_len, cos_t, sin_t):
    b, lp, _ = p3d.shape
    tq = _attn_tile(seq_len)
    lam_init = 0.8 - 0.6 * math.exp(-0.3 * layer)
    g0 = 5 * AT_HEADS
    col = lambda g: pl.BlockSpec((1, lp, LANES), lambda i, h, g=g: (i, 0, g0 + g * AT_HEADS + h))
    const = lambda shape: pl.BlockSpec(shape, lambda i, h: (0,) * len(shape))
    return pl.pallas_call(
        functools.partial(_attn_kernel, tq=tq, lam_init=lam_init, seq_len=seq_len),
        out_shape=jax.ShapeDtypeStruct((b, lp, AT_HEADS * 2 * AT_HEAD_DIM), BF16),
        grid=(b, AT_HEADS),
        in_specs=[
            col(0), col(1), col(2), col(3),
            const((lp, LANES)), const((lp, LANES)),
            const((4, AT_HEAD_DIM)), const((1, LANES)),
        ],
        out_specs=pl.BlockSpec((1, lp, LANES), lambda i, h: (i, 0, h)),
        scratch_shapes=[
            pltpu.VMEM((2, lp, LANES), BF16),
            pltpu.VMEM((lp, 2 * LANES), BF16),
            pltpu.VMEM((2, 2, tq, lp), F32),
            pltpu.VMEM((2, 2, tq, 1), F32),
        ],
        compiler_params=pltpu.CompilerParams(
            dimension_semantics=("parallel", "parallel"), vmem_limit_bytes=VMEM_LIMIT),
        name="diff_attn",
    )(p3d, p3d, p3d, p3d, cos_t, sin_t, lam_vecs, gain.reshape(1, LANES))


def _outproj_kernel(oh_ref, oa_ref, w_ref, h_ref, fg_ref, out_ref, *, final):
    wh = oh_ref.shape[1]
    acc = h_ref[...] + _dot(oh_ref[...], w_ref[0:wh, :]) + _dot(oa_ref[...], w_ref[wh:, :])
    if final:
        ms = jnp.mean(acc * acc, axis=-1, keepdims=True)
        acc = acc * lax.rsqrt(ms + NORM_EPS) * fg_ref[...]
    out_ref[...] = acc


def _outproj(o_h, o_a, w_bf16, layer, h3d, final_g, rows, final):
    b, _, d = h3d.shape
    wh, wa = o_h.shape[2], o_a.shape[2]
    tm = _pick_tile(rows, 576, 16)
    row_block = lambda w: pl.BlockSpec((None, tm, w), lambda i, j: (i, j, 0))
    return pl.pallas_call(
        functools.partial(_outproj_kernel, final=final),
        out_shape=jax.ShapeDtypeStruct((b, rows, d), F32),
        grid=(b, rows // tm),
        in_specs=[
            row_block(wh), row_block(wa),
            pl.BlockSpec((None, wh + wa, d), lambda i, j: (layer, 0, 0)),
            row_block(d),
            pl.BlockSpec((1, d), lambda i, j: (0, 0)),
        ],
        out_specs=row_block(d),
        compiler_params=pltpu.CompilerParams(
            dimension_semantics=("parallel", "parallel"), vmem_limit_bytes=VMEM_LIMIT),
        name="outproj",
    )(o_h, o_a, w_bf16, h3d, final_g.reshape(1, d))


def kernel(x, meta_tokens, norm_g, w_in, hg_lb_logits, hg_norm_g, diff_lambda, diff_subln_g,
           w_out, final_norm_g):
    bsz, seq, d = x.shape
    depth = w_in.shape[0]
    seq_len = N_META + seq
    lp = seq + HG_CHUNK
    assert seq % HG_CHUNK == 0 and N_META <= HG_CHUNK
    assert w_in.shape[2] == N_GROUPS * HG_HEADS * LANES

    meta = jnp.broadcast_to(meta_tokens[None].astype(x.dtype), (bsz, N_META, d))
    pad = jnp.zeros((bsz, lp - seq_len, d), x.dtype)
    h = jnp.concatenate([x, meta, pad], axis=1)
    positions = jnp.concatenate([jnp.arange(N_META, seq_len), jnp.arange(N_META),
                                 jnp.zeros((lp - seq_len,), jnp.int32)])
    cos_t, sin_t = _rope_tables(positions)
    w_in_bf = w_in.astype(BF16)
    w_out_bf = w_out.astype(BF16)

    for l in range(depth):
        final = l == depth - 1
        p = _norm_inproj(h.reshape(bsz * lp, d), norm_g[l], w_in_bf, l).reshape(bsz, lp, -1)
        o_h = _hgrn(p, hg_lb_logits, hg_norm_g[l], l, seq_len)
        o_a = _attn(p, diff_lambda[l], diff_subln_g[l], l, seq_len, cos_t, sin_t)
        h = _outproj(o_h, o_a, w_out_bf, l, h, final_norm_g, seq if final else lp, final)
    return h
```

```python
import functools
import math

import numpy as np
import jax
import jax.numpy as jnp
from jax import lax
from jax.experimental import pallas as pl
from jax.experimental.pallas import tpu as pltpu

F32 = jnp.float32
BF16 = jnp.bfloat16

N_META = 16
HG_HEADS = 8
HG_KDIM = 128
HG_VDIM = 128
AT_HEADS = 8
AT_HEAD_DIM = 64
ROPE_THETA = 10000.0
NORM_EPS = 1e-6
F_FLOOR = 1e-30
N_GROUPS = 9
LOG2E = math.log2(math.e)

LANES = 128
SUBLANES = 8
VMEM_LIMIT = 56 * 1024 * 1024

HG_CHUNK = 128
HG_LEVELS = (64, 32, 16, 8, 4, 2, 1)
HG_DIAG_LEVEL = len(HG_LEVELS)
HG_HEADS_PER_STEP = 4

KEY_MASK = -1e30


def _pick_tile(n, cap, mult):
    best = None
    for t in range(mult, min(n, cap) + 1, mult):
        if n % t == 0:
            best = t
    assert best is not None, (n, cap, mult)
    return best


def _dot(a, b):
    return jnp.dot(a, b, preferred_element_type=F32)


def _dot_nt(a, b):
    return lax.dot_general(a, b, (((1,), (1,)), ((), ())), preferred_element_type=F32)


def _silu(g):
    return g * (1.0 / (1.0 + jnp.exp(-g)))


def _norm_inproj_kernel(h_ref, g_ref, w_ref, p_ref, u_ref):
    @pl.when(pl.program_id(1) == 0)
    def _():
        x = h_ref[...]
        ms = jnp.mean(x * x, axis=-1, keepdims=True)
        u_ref[...] = (x * lax.rsqrt(ms + NORM_EPS) * g_ref[...]).astype(BF16)

    p_ref[...] = _dot(u_ref[...], w_ref[...]).astype(BF16)


def _norm_inproj(h2d, g, w_bf16, layer):
    m, d = h2d.shape
    n = w_bf16.shape[2]
    tm = _pick_tile(m, 2304, LANES)
    tn = _pick_tile(n, 1024, LANES)
    return pl.pallas_call(
        _norm_inproj_kernel,
        out_shape=jax.ShapeDtypeStruct((m, n), BF16),
        grid=(m // tm, n // tn),
        in_specs=[
            pl.BlockSpec((tm, d), lambda i, j: (i, 0)),
            pl.BlockSpec((1, d), lambda i, j: (0, 0)),
            pl.BlockSpec((None, d, tn), lambda i, j: (layer, 0, j)),
        ],
        out_specs=pl.BlockSpec((tm, tn), lambda i, j: (i, j)),
        scratch_shapes=[pltpu.VMEM((tm, d), BF16)],
        compiler_params=pltpu.CompilerParams(
            dimension_semantics=("parallel", "arbitrary"), vmem_limit_bytes=VMEM_LIMIT),
        name="norm_inproj",
    )(h2d, g.reshape(1, d), w_bf16)


def _hgrn_constants(last_valid):
    c = HG_CHUNK
    t = np.arange(c)[:, None]
    s = np.arange(c)[None, :]
    tril = (s <= t).astype(np.float32)
    real = (np.arange(c) < last_valid).astype(np.float32)[None, :]
    tril = np.stack([tril, tril * real])
    triu = np.stack([tril[0].T, tril[0].T * real])
    x = t ^ s
    top = np.where(x > 0, np.floor(np.log2(np.maximum(x, 1))).astype(np.int64), 0)
    half = 1 << top
    level = np.full((c, c), -1, np.int32)
    for li, hh in enumerate(HG_LEVELS):
        level = np.where((t > s) & (half == hh), li, level)
    level = np.where(t == s, HG_DIAG_LEVEL, level).astype(np.int32)
    return (jnp.asarray(tril, BF16), jnp.asarray(triu, BF16),
            jnp.asarray(level), jnp.asarray(level.T))


def _split3(x):
    h1 = x.astype(BF16)
    r1 = x - h1.astype(F32)
    h2 = r1.astype(BF16)
    r2 = r1 - h2.astype(F32)
    return h1, h2, r2.astype(BF16)


def _level_operands(q, kin, cum, cum_ref, half, rev):
    c = HG_CHUNK
    blk = 2 * half
    off = half if rev else half - 1
    zeros = jnp.zeros((half, LANES), F32)
    qt, kt = [], []
    for m in range(c // blk):
        ref = cum_ref[pl.ds(m * blk + off, 1), :]
        lo = slice(m * blk, m * blk + half)
        hi = slice(m * blk + half, (m + 1) * blk)
        if rev:
            qt += [q[lo] * jnp.exp2(cum[lo] - ref), zeros]
            kt += [zeros, kin[hi] * jnp.exp2(ref - cum[hi])]
        else:
            kt += [kin[lo] * jnp.exp2(ref - cum[lo]), zeros]
            qt += [zeros, q[hi] * jnp.exp2(cum[hi] - ref)]
    return jnp.concatenate(qt, axis=0), jnp.concatenate(kt, axis=0)


def _level_exponent(cum, cum_ref, half, rev):
    c = HG_CHUNK
    blk = 2 * half
    off = half if rev else half - 1
    parts = []
    assert half < SUBLANES
    sub = lax.broadcasted_iota(jnp.int32, (SUBLANES, LANES), 0)
    for g in range(c // SUBLANES):
        base = g * SUBLANES
        ref = jnp.broadcast_to(cum_ref[pl.ds(base + off, 1), :], (SUBLANES, LANES))
        if blk < SUBLANES:
            hi = jnp.broadcast_to(cum_ref[pl.ds(base + blk + off, 1), :], (SUBLANES, LANES))
            ref = jnp.where(sub < blk, ref, hi)
        parts.append(-jnp.abs(cum[base:base + SUBLANES] - ref))
    return jnp.concatenate(parts, axis=0)


def _hgrn_chunks(chains):
    c = HG_CHUNK
    row = lax.broadcasted_iota(jnp.int32, (c, LANES), 0)
    for ch in chains:
        sig = 1.0 / (1.0 + jnp.exp(-ch["z"]))
        f = ch["lb"] + (1.0 - ch["lb"]) * sig
        ch["kin"] = 1.0 - f
        ch["f"] = jnp.maximum(f, F_FLOOR)
        ch["parts"] = _split3(jnp.log2(ch["f"]))
    for ch in chains:
        h1, h2, h3 = ch["parts"]
        cum = _dot(ch["tri"], h1) + _dot(ch["tri"], h2) + _dot(ch["tri"], h3)
        ch["cum_ref"][...] = cum
        ch["cum"] = cum
        ch["tot"] = cum[0:1] if ch["rev"] else cum[c - 1:c]
    for ch in chains:
        cum, tot = ch["cum"], ch["tot"]
        q_dec = ch["q"] * jnp.exp2(cum)
        k_dec = ch["kin"] * jnp.exp2(tot - cum)
        st = ch["st_ref"][...]
        ch["o"] = _dot_nt(q_dec.astype(BF16), st.astype(BF16))
        ch["v_bf"] = ch["v"].astype(BF16)
        ch["st_ref"][...] = st * jnp.exp2(tot) + _dot(ch["v"].T.astype(BF16), k_dec.astype(BF16))
        ch["att"] = jnp.where(ch["level"] == HG_DIAG_LEVEL,
                              _dot_nt(ch["q"].astype(BF16), ch["kin"].astype(BF16)), 0.0)
    for li, half in enumerate(HG_LEVELS):
        for ch in chains:
            if half >= SUBLANES:
                qt, kt = _level_operands(ch["q"], ch["kin"], ch["cum"], ch["cum_ref"], half, ch["rev"])
            else:
                if half == 1:
                    q_side = (row % 2 == 0) if ch["rev"] else (row % 2 == 1)
                    ex = jnp.where(q_side, ch["f"], 1.0)
                else:
                    ex = jnp.exp2(_level_exponent(ch["cum"], ch["cum_ref"], half, ch["rev"]))
                qt, kt = ch["q"] * ex, ch["kin"] * ex
            a = _dot_nt(qt.astype(BF16), kt.astype(BF16))
            ch["att"] = jnp.where(ch["level"] == li, a, ch["att"])
    return [ch["o"] + _dot(ch["att"].astype(BF16), ch["v_bf"]) for ch in chains]


def _hgrn_kernel(q_ref, zf_ref, zb_ref, v_ref, gate_ref, lbl_ref, gain_ref,
                 tril_ref, triu_ref, lvf_ref, lvb_ref, o_ref,
                 of_s, ob_s, stf_s, stb_s, cumf_s, cumb_s, *, layer, seq_len):
    c = HG_CHUNK
    lp = q_ref.shape[1]
    nc = lp // c
    depth = lbl_ref.shape[0]
    nheads = q_ref.shape[2] // LANES

    logits = [lbl_ref[d] for d in range(depth)]
    mx = functools.reduce(jnp.maximum, logits)
    es = [jnp.exp(x - mx) for x in logits]
    den = functools.reduce(lambda a, b: a + b, es)
    num = jnp.zeros_like(den)
    for d in range(1, layer + 1):
        num = num + es[d]
    lbs = num / den

    stf_s[...] = jnp.zeros_like(stf_s)
    stb_s[...] = jnp.zeros_like(stb_s)

    def body(n, carry):
        cf = lax.rem(n + nc - 1, nc)
        cb = lax.rem(2 * nc - 2 - n, nc)
        rf = pl.ds(pl.multiple_of(cf * c, c), c)
        rb = pl.ds(pl.multiple_of(cb * c, c), c)
        tri_f = tril_ref[(cf == nc - 1).astype(jnp.int32)]
        tri_b = triu_ref[(cb == nc - 1).astype(jnp.int32)]
        chains = []
        for hh in range(nheads):
            ln = pl.ds(hh * LANES, LANES)
            lanes = slice(hh * LANES, (hh + 1) * LANES)
            chains.append(dict(
                q=q_ref[0, rf, ln].astype(F32), z=zf_ref[0, rf, ln].astype(F32), v=v_ref[0, rf, ln].astype(F32),
                lb=lbs[0:1, lanes], st_ref=stf_s.at[hh], cum_ref=cumf_s.at[hh],
                tri=tri_f, level=lvf_ref[...], rev=False, out=(of_s, rf, ln)))
            chains.append(dict(
                q=q_ref[0, rb, ln].astype(F32), z=zb_ref[0, rb, ln].astype(F32), v=v_ref[0, rb, ln].astype(F32),
                lb=lbs[1:2, lanes], st_ref=stb_s.at[hh], cum_ref=cumb_s.at[hh],
                tri=tri_b, level=lvb_ref[...], rev=True, out=(ob_s, rb, ln)))
        for ch, o in zip(chains, _hgrn_chunks(chains)):
            dst, rows, ln = ch["out"]
            dst[rows, ln] = o
        return carry

    lax.fori_loop(0, nc, body, 0)

    def epilogue(n, carry):
        rows = pl.ds(pl.multiple_of(n * c, c), c)
        pos = n * c + lax.broadcasted_iota(jnp.int32, (c, LANES), 0)
        for hh in range(nheads):
            ln = pl.ds(hh * LANES, LANES)
            o = of_s[rows, ln] + ob_s[rows, ln]
            ms = jnp.mean(o * o, axis=-1, keepdims=True)
            y = o * lax.rsqrt(ms + NORM_EPS) * gain_ref[...] * _silu(gate_ref[0, rows, ln].astype(F32))
            o_ref[0, rows, ln] = jnp.where(pos < seq_len, y, 0.0).astype(o_ref.dtype)
        return carry

    lax.fori_loop(0, nc, epilogue, 0)


def _hgrn(p3d, lb_logits, gain, layer, seq_len):
    b, lp, _ = p3d.shape
    c = HG_CHUNK
    depth = lb_logits.shape[0]
    hps = HG_HEADS_PER_STEP
    wid = hps * LANES
    ngrp = HG_HEADS // hps
    tril, triu, lvf, lvb = _hgrn_constants(seq_len - (lp - c))
    col = lambda g: pl.BlockSpec((1, lp, wid), lambda i, h, g=g: (i, 0, g * ngrp + h))
    const = lambda shape: pl.BlockSpec(shape, lambda i, h: (0,) * len(shape))
    return pl.pallas_call(
        functools.partial(_hgrn_kernel, layer=layer, seq_len=seq_len),
        out_shape=jax.ShapeDtypeStruct((b, lp, HG_HEADS * HG_VDIM), BF16),
        grid=(b, ngrp),
        in_specs=[
            col(0), col(1), col(2), col(3), col(4),
            pl.BlockSpec((depth, 2, wid), lambda i, h: (0, 0, h)),
            const((1, HG_VDIM)),
            const((2, c, c)), const((2, c, c)), const((c, c)), const((c, c)),
        ],
        out_specs=pl.BlockSpec((1, lp, wid), lambda i, h: (i, 0, h)),
        scratch_shapes=[
            pltpu.VMEM((lp, wid), F32), pltpu.VMEM((lp, wid), F32),
            pltpu.VMEM((hps, HG_VDIM, HG_KDIM), F32), pltpu.VMEM((hps, HG_VDIM, HG_KDIM), F32),
            pltpu.VMEM((hps, c, HG_KDIM), F32), pltpu.VMEM((hps, c, HG_KDIM), F32),
        ],
        compiler_params=pltpu.CompilerParams(
            dimension_semantics=("parallel", "parallel"), vmem_limit_bytes=VMEM_LIMIT),
        name="hgrn2",
    )(p3d, p3d, p3d, p3d, p3d, lb_logits, gain.reshape(1, HG_VDIM), tril, triu, lvf, lvb)


def _rope_tables(positions):
    half = AT_HEAD_DIM // 2
    inv = ROPE_THETA ** (-jnp.arange(0, AT_HEAD_DIM, 2, dtype=F32) / AT_HEAD_DIM)
    ang = positions.astype(F32)[:, None] * inv[None, :]
    cos, sin = jnp.cos(ang), jnp.sin(ang)
    reps = LANES // half
    cos_t = jnp.tile(cos, (1, reps))
    sign = jnp.where((jnp.arange(LANES) % AT_HEAD_DIM) < half, -1.0, 1.0).astype(F32)
    sin_t = jnp.tile(sin, (1, reps)) * sign[None, :]
    return cos_t, sin_t


def _rope(t, cos_t, sin_t):
    half = AT_HEAD_DIM // 2
    lane = lax.broadcasted_iota(jnp.int32, t.shape, 1)
    first = (lane % AT_HEAD_DIM) < half
    partner = jnp.where(first, pltpu.roll(t, LANES - half, axis=1), pltpu.roll(t, half, axis=1))
    return t * cos_t + partner * sin_t


AT_TQ_CAP = 704


def _attn_tile(seq_len):
    best = None
    for t in range(16, min(seq_len, AT_TQ_CAP) + 1, 16):
        if seq_len % t == 0:
            best = t
    assert best is not None, seq_len
    return best


_BIAS_LANE = (AT_HEAD_DIM, 0)


def _attn_kernel(q_ref, k_ref, v_ref, gate_ref, cos_ref, sin_ref, lamv_ref, gain_ref, o_ref,
                 k_s, vaug_s, s_s, m_s, *, tq, lam_init, seq_len):
    lp = k_ref.shape[1]
    nq = seq_len // tq

    tk = _pick_tile(lp, AT_TQ_CAP, LANES)
    for n in range(lp // tk):
        rows = pl.ds(n * tk, tk)
        kr = _rope(k_ref[0, rows, :].astype(F32), cos_ref[rows, :], sin_ref[rows, :])
        lane = lax.broadcasted_iota(jnp.int32, (tk, LANES), 1)
        pos = n * tk + lax.broadcasted_iota(jnp.int32, (tk, LANES), 0)
        bias = jnp.where(pos < seq_len, 0.0, KEY_MASK)
        for c in range(2):
            in_half = (lane < AT_HEAD_DIM) if c == 0 else (lane >= AT_HEAD_DIM)
            kc = jnp.where(in_half, kr, jnp.where(lane == _BIAS_LANE[c], bias, 0.0))
            k_s[c, rows, :] = kc.astype(BF16)
        vaug_s[rows, 0:LANES] = v_ref[0, rows, :]
        vaug_s[rows, LANES:2 * LANES] = jnp.ones((tk, LANES), BF16)

    lv = lamv_ref[...]
    lam = (jnp.exp(jnp.sum(lv[0:1] * lv[1:2], axis=-1, keepdims=True))
           - jnp.exp(jnp.sum(lv[2:3] * lv[3:4], axis=-1, keepdims=True)) + lam_init)

    def scores(j, slot):
        rows = pl.ds(j * tq, tq)
        q = _rope(q_ref[0, rows, :].astype(F32), cos_ref[rows, :], sin_ref[rows, :])
        q = q * (AT_HEAD_DIM ** -0.5 * LOG2E)
        lane = lax.broadcasted_iota(jnp.int32, (tq, LANES), 1)
        for c in range(2):
            in_half = (lane < AT_HEAD_DIM) if c == 0 else (lane >= AT_HEAD_DIM)
            qc = jnp.where(in_half, q, jnp.where(lane == _BIAS_LANE[c], 1.0, 0.0)).astype(BF16)
            s = _dot_nt(qc, k_s[c])
            s_s[slot, c] = s
            m_s[slot, c] = jnp.max(s, axis=-1, keepdims=True)

    def outputs(j, slot):
        rows = pl.ds(j * tq, tq)
        outs = []
        for c in range(2):
            p = jnp.exp2(s_s[slot, c] - m_s[slot, c]).astype(BF16)
            ol = _dot(p, vaug_s[...])
            outs.append(ol[:, 0:LANES] / ol[:, LANES:2 * LANES])
        o = outs[0] - lam * outs[1]
        ms = jnp.mean(o * o, axis=-1, keepdims=True)
        y = (o * lax.rsqrt(ms + NORM_EPS) * gain_ref[...] * (1.0 - lam_init)
             * _silu(gate_ref[0, rows, :].astype(F32)))
        o_ref[0, rows, :] = y.astype(o_ref.dtype)

    scores(0, 0)
    for j in range(nq):
        if j + 1 < nq:
            scores(j + 1, (j + 1) % 2)
        outputs(j, j % 2)
    if seq_len < lp:
        o_ref[0, seq_len:lp, :] = jnp.zeros((lp - seq_len, LANES), o_ref.dtype)


def _attn(p3d, lam_vecs, gain, layer, seq_len, cos_t, sin_t):
    b, lp, _ = p3d.shape
    tq = _attn_tile(seq_len)
    lam_init = 0.8 - 0.6 * math.exp(-0.3 * layer)
    g0 = 5 * AT_HEADS
    col = lambda g: pl.BlockSpec((1, lp, LANES), lambda i, h, g=g: (i, 0, g0 + g * AT_HEADS + h))
    const = lambda shape: pl.BlockSpec(shape, lambda i, h: (0,) * len(shape))
    return pl.pallas_call(
        functools.partial(_attn_kernel, tq=tq, lam_init=lam_init, seq_len=seq_len),
        out_shape=jax.ShapeDtypeStruct((b, lp, AT_HEADS * 2 * AT_HEAD_DIM), BF16),
        grid=(b, AT_HEADS),
        in_specs=[
            col(0), col(1), col(2), col(3),
            const((lp, LANES)), const((lp, LANES)),
            const((4, AT_HEAD_DIM)), const((1, LANES)),
        ],
        out_specs=pl.BlockSpec((1, lp, LANES), lambda i, h: (i, 0, h)),
        scratch_shapes=[
            pltpu.VMEM((2, lp, LANES), BF16),
            pltpu.VMEM((lp, 2 * LANES), BF16),
            pltpu.VMEM((2, 2, tq, lp), F32),
            pltpu.VMEM((2, 2, tq, 1), F32),
        ],
        compiler_params=pltpu.CompilerParams(
            dimension_semantics=("parallel", "parallel"), vmem_limit_bytes=VMEM_LIMIT),
        name="diff_attn",
    )(p3d, p3d, p3d, p3d, cos_t, sin_t, lam_vecs, gain.reshape(1, LANES))


def _outproj_kernel(oh_ref, oa_ref, w_ref, h_ref, fg_ref, out_ref, *, final):
    wh = oh_ref.shape[1]
    acc = h_ref[...] + _dot(oh_ref[...], w_ref[0:wh, :]) + _dot(oa_ref[...], w_ref[wh:, :])
    if final:
        ms = jnp.mean(acc * acc, axis=-1, keepdims=True)
        acc = acc * lax.rsqrt(ms + NORM_EPS) * fg_ref[...]
    out_ref[...] = acc


def _outproj(o_h, o_a, w_bf16, layer, h3d, final_g, rows, final):
    b, _, d = h3d.shape
    wh, wa = o_h.shape[2], o_a.shape[2]
    tm = _pick_tile(rows, 576, 16)
    row_block = lambda w: pl.BlockSpec((None, tm, w), lambda i, j: (i, j, 0))
    return pl.pallas_call(
        functools.partial(_outproj_kernel, final=final),
        out_shape=jax.ShapeDtypeStruct((b, rows, d), F32),
        grid=(b, rows // tm),
        in_specs=[
            row_block(wh), row_block(wa),
            pl.BlockSpec((None, wh + wa, d), lambda i, j: (layer, 0, 0)),
            row_block(d),
            pl.BlockSpec((1, d), lambda i, j: (0, 0)),
        ],
        out_specs=row_block(d),
        compiler_params=pltpu.CompilerParams(
            dimension_semantics=("parallel", "parallel"), vmem_limit_bytes=VMEM_LIMIT),
        name="outproj",
    )(o_h, o_a, w_bf16, h3d, final_g.reshape(1, d))


def kernel(x, meta_tokens, norm_g, w_in, hg_lb_logits, hg_norm_g, diff_lambda, diff_subln_g,
           w_out, final_norm_g):
    bsz, seq, d = x.shape
    depth = w_in.shape[0]
    seq_len = N_META + seq
    lp = seq + HG_CHUNK
    assert seq % HG_CHUNK == 0 and N_META <= HG_CHUNK
    assert w_in.shape[2] == N_GROUPS * HG_HEADS * LANES

    meta = jnp.broadcast_to(meta_tokens[None].astype(x.dtype), (bsz, N_META, d))
    pad = jnp.zeros((bsz, lp - seq_len, d), x.dtype)
    h = jnp.concatenate([x, meta, pad], axis=1)
    positions = jnp.concatenate([jnp.arange(N_META, seq_len), jnp.arange(N_META),
                                 jnp.zeros((lp - seq_len,), jnp.int32)])
    cos_t, sin_t = _rope_tables(positions)
    w_in_bf = w_in.astype(BF16)
    w_out_bf = w_out.astype(BF16)

    for l in range(depth):
        final = l == depth - 1
        p = _norm_inproj(h.reshape(bsz * lp, d), norm_g[l], w_in_bf, l).reshape(bsz, lp, -1)
        o_h = _hgrn(p, hg_lb_logits, hg_norm_g[l], l, seq_len)
        o_a = _attn(p, diff_lambda[l], diff_subln_g[l], l, seq_len, cos_t, sin_t)
        h = _outproj(o_h, o_a, w_out_bf, l, h, final_norm_g, seq if final else lp, final)
    return h
```

```python
import functools
import math

import numpy as np
import jax
import jax.numpy as jnp
from jax import lax
from jax.experimental import pallas as pl
from jax.experimental.pallas import tpu as pltpu

F32 = jnp.float32
BF16 = jnp.bfloat16

N_META = 16
HG_HEADS = 8
HG_KDIM = 128
HG_VDIM = 128
AT_HEADS = 8
AT_HEAD_DIM = 64
ROPE_THETA = 10000.0
NORM_EPS = 1e-6
F_FLOOR = 1e-30
N_GROUPS = 9
LOG2E = math.log2(math.e)

LANES = 128
SUBLANES = 8
VMEM_LIMIT = 56 * 1024 * 1024

HG_CHUNK = 128
HG_LEVELS = (64, 32, 16, 8, 4, 2, 1)
HG_DIAG_LEVEL = len(HG_LEVELS)
HG_HEADS_PER_STEP = 4

KEY_MASK = -1e30
INPROJ_TM_CAP = 2304
INPROJ_TN_CAP = 1024
OUTPROJ_TM_CAP = 576


def _pick_tile(n, cap, mult):
    best = None
    for t in range(mult, min(n, cap) + 1, mult):
        if n % t == 0:
            best = t
    assert best is not None, (n, cap, mult)
    return best


def _dot(a, b):
    return jnp.dot(a, b, preferred_element_type=F32)


def _dot_nt(a, b):
    return lax.dot_general(a, b, (((1,), (1,)), ((), ())), preferred_element_type=F32)


def _silu(g):
    return g * (1.0 / (1.0 + jnp.exp(-g)))


def _norm_inproj_kernel(h_ref, g_ref, w_ref, p_ref, u_ref):
    @pl.when(pl.program_id(1) == 0)
    def _():
        x = h_ref[...]
        ms = jnp.mean(x * x, axis=-1, keepdims=True)
        u_ref[...] = (x * lax.rsqrt(ms + NORM_EPS) * g_ref[...]).astype(BF16)

    p_ref[...] = _dot(u_ref[...], w_ref[...]).astype(BF16)


def _norm_inproj(h2d, g, w_bf16, layer):
    m, d = h2d.shape
    n = w_bf16.shape[2]
    tm = _pick_tile(m, INPROJ_TM_CAP, LANES)
    tn = _pick_tile(n, INPROJ_TN_CAP, LANES)
    return pl.pallas_call(
        _norm_inproj_kernel,
        out_shape=jax.ShapeDtypeStruct((m, n), BF16),
        grid=(m // tm, n // tn),
        in_specs=[
            pl.BlockSpec((tm, d), lambda i, j: (i, 0)),
            pl.BlockSpec((1, d), lambda i, j: (0, 0)),
            pl.BlockSpec((None, d, tn), lambda i, j: (layer, 0, j)),
        ],
        out_specs=pl.BlockSpec((tm, tn), lambda i, j: (i, j)),
        scratch_shapes=[pltpu.VMEM((tm, d), BF16)],
        compiler_params=pltpu.CompilerParams(
            dimension_semantics=("parallel", "arbitrary"), vmem_limit_bytes=VMEM_LIMIT),
        name="norm_inproj",
    )(h2d, g.reshape(1, d), w_bf16)


def _hgrn_constants(last_valid):
    c = HG_CHUNK
    t = np.arange(c)[:, None]
    s = np.arange(c)[None, :]
    tril = (s <= t).astype(np.float32)
    real = (np.arange(c) < last_valid).astype(np.float32)[None, :]
    tril = np.stack([tril, tril * real])
    triu = np.stack([tril[0].T, tril[0].T * real])
    x = t ^ s
    top = np.where(x > 0, np.floor(np.log2(np.maximum(x, 1))).astype(np.int64), 0)
    half = 1 << top
    level = np.full((c, c), -1, np.int32)
    for li, hh in enumerate(HG_LEVELS):
        level = np.where((t > s) & (half == hh), li, level)
    level = np.where(t == s, HG_DIAG_LEVEL, level).astype(np.int32)
    return (jnp.asarray(tril, BF16), jnp.asarray(triu, BF16),
            jnp.asarray(level), jnp.asarray(level.T))


def _split2(x):
    hi = x.astype(BF16)
    return hi, (x - hi.astype(F32)).astype(BF16)


def _level_operands(q, kin, cum, cum_ref, half, rev):
    c = HG_CHUNK
    blk = 2 * half
    off = half if rev else half - 1
    zeros = jnp.zeros((half, LANES), F32)
    qt, kt = [], []
    for m in range(c // blk):
        ref = cum_ref[pl.ds(m * blk + off, 1), :]
        lo = slice(m * blk, m * blk + half)
        hi = slice(m * blk + half, (m + 1) * blk)
        if rev:
            qt += [q[lo] * jnp.exp2(cum[lo] - ref), zeros]
            kt += [zeros, kin[hi] * jnp.exp2(ref - cum[hi])]
        else:
            kt += [kin[lo] * jnp.exp2(ref - cum[lo]), zeros]
            qt += [zeros, q[hi] * jnp.exp2(cum[hi] - ref)]
    return jnp.concatenate(qt, axis=0), jnp.concatenate(kt, axis=0)


def _level_exponent(cum, cum_ref, half, rev):
    c = HG_CHUNK
    blk = 2 * half
    off = half if rev else half - 1
    parts = []
    assert half < SUBLANES
    sub = lax.broadcasted_iota(jnp.int32, (SUBLANES, LANES), 0)
    for g in range(c // SUBLANES):
        base = g * SUBLANES
        ref = jnp.broadcast_to(cum_ref[pl.ds(base + off, 1), :], (SUBLANES, LANES))
        if blk < SUBLANES:
            hi = jnp.broadcast_to(cum_ref[pl.ds(base + blk + off, 1), :], (SUBLANES, LANES))
            ref = jnp.where(sub < blk, ref, hi)
        parts.append(-jnp.abs(cum[base:base + SUBLANES] - ref))
    return jnp.concatenate(parts, axis=0)


def _hgrn_chunks(chains):
    c = HG_CHUNK
    row = lax.broadcasted_iota(jnp.int32, (c, LANES), 0)
    for ch in chains:
        sig = 1.0 / (1.0 + jnp.exp(-ch["z"]))
        f = ch["lb"] + (1.0 - ch["lb"]) * sig
        ch["kin"] = 1.0 - f
        ch["f"] = jnp.maximum(f, F_FLOOR)
        ch["parts"] = _split2(jnp.log(ch["f"]))
    for ch in chains:
        hi, lo = ch["parts"]
        cum = (_dot(ch["tri"], hi) + _dot(ch["tri"], lo)) * LOG2E
        ch["cum_ref"][...] = cum
        ch["cum"] = cum
        ch["tot"] = cum[0:1] if ch["rev"] else cum[c - 1:c]
    for ch in chains:
        cum, tot = ch["cum"], ch["tot"]
        q_dec = ch["q"] * jnp.exp2(cum)
        k_dec = ch["kin"] * jnp.exp2(tot - cum)
        st = ch["st_ref"][...]
        ch["o"] = _dot_nt(q_dec.astype(BF16), st.astype(BF16))
        ch["v_bf"] = ch["v"].astype(BF16)
        ch["st_ref"][...] = st * jnp.exp2(tot) + _dot(ch["v"].T.astype(BF16), k_dec.astype(BF16))
        ch["att"] = jnp.where(ch["level"] == HG_DIAG_LEVEL,
                              _dot_nt(ch["q"].astype(BF16), ch["kin"].astype(BF16)), 0.0)
    for li, half in enumerate(HG_LEVELS):
        for ch in chains:
            if half >= SUBLANES:
                qt, kt = _level_operands(ch["q"], ch["kin"], ch["cum"], ch["cum_ref"], half, ch["rev"])
            else:
                if half == 1:
                    q_side = (row % 2 == 0) if ch["rev"] else (row % 2 == 1)
                    ex = jnp.where(q_side, ch["f"], 1.0)
                else:
                    ex = jnp.exp2(_level_exponent(ch["cum"], ch["cum_ref"], half, ch["rev"]))
                qt, kt = ch["q"] * ex, ch["kin"] * ex
            a = _dot_nt(qt.astype(BF16), kt.astype(BF16))
            ch["att"] = jnp.where(ch["level"] == li, a, ch["att"])
    return [ch["o"] + _dot(ch["att"].astype(BF16), ch["v_bf"]) for ch in chains]


def _hgrn_kernel(q_ref, zf_ref, zb_ref, v_ref, gate_ref, lbl_ref, gain_ref,
                 tril_ref, triu_ref, lvf_ref, lvb_ref, o_ref,
                 of_s, ob_s, stf_s, stb_s, cumf_s, cumb_s, *, layer, seq_len):
    c = HG_CHUNK
    lp = q_ref.shape[1]
    nc = lp // c
    depth = lbl_ref.shape[0]
    nheads = q_ref.shape[2] // LANES

    logits = [lbl_ref[d] for d in range(depth)]
    mx = functools.reduce(jnp.maximum, logits)
    es = [jnp.exp(x - mx) for x in logits]
    den = functools.reduce(lambda a, b: a + b, es)
    num = jnp.zeros_like(den)
    for d in range(1, layer + 1):
        num = num + es[d]
    lbs = num / den

    stf_s[...] = jnp.zeros_like(stf_s)
    stb_s[...] = jnp.zeros_like(stb_s)

    def body(n, carry):
        cf = lax.rem(n + nc - 1, nc)
        cb = lax.rem(2 * nc - 2 - n, nc)
        rf = pl.ds(pl.multiple_of(cf * c, c), c)
        rb = pl.ds(pl.multiple_of(cb * c, c), c)
        tri_f = tril_ref[(cf == nc - 1).astype(jnp.int32)]
        tri_b = triu_ref[(cb == nc - 1).astype(jnp.int32)]
        chains = []
        for hh in range(nheads):
            ln = pl.ds(hh * LANES, LANES)
            lanes = slice(hh * LANES, (hh + 1) * LANES)
            chains.append(dict(
                q=q_ref[0, rf, ln].astype(F32), z=zf_ref[0, rf, ln].astype(F32), v=v_ref[0, rf, ln].astype(F32),
                lb=lbs[0:1, lanes], st_ref=stf_s.at[hh], cum_ref=cumf_s.at[hh],
                tri=tri_f, level=lvf_ref[...], rev=False, out=(of_s, rf, ln)))
            chains.append(dict(
                q=q_ref[0, rb, ln].astype(F32), z=zb_ref[0, rb, ln].astype(F32), v=v_ref[0, rb, ln].astype(F32),
                lb=lbs[1:2, lanes], st_ref=stb_s.at[hh], cum_ref=cumb_s.at[hh],
                tri=tri_b, level=lvb_ref[...], rev=True, out=(ob_s, rb, ln)))
        for ch, o in zip(chains, _hgrn_chunks(chains)):
            dst, rows, ln = ch["out"]
            dst[rows, ln] = o
        return carry

    lax.fori_loop(0, nc, body, 0)

    def epilogue(n, carry):
        rows = pl.ds(pl.multiple_of(n * c, c), c)
        pos = n * c + lax.broadcasted_iota(jnp.int32, (c, LANES), 0)
        for hh in range(nheads):
            ln = pl.ds(hh * LANES, LANES)
            o = of_s[rows, ln] + ob_s[rows, ln]
            ms = jnp.mean(o * o, axis=-1, keepdims=True)
            y = o * lax.rsqrt(ms + NORM_EPS) * gain_ref[...] * _silu(gate_ref[0, rows, ln].astype(F32))
            o_ref[0, rows, ln] = jnp.where(pos < seq_len, y, 0.0).astype(o_ref.dtype)
        return carry

    lax.fori_loop(0, nc, epilogue, 0)


def _hgrn(p3d, lb_logits, gain, layer, seq_len):
    b, lp, _ = p3d.shape
    c = HG_CHUNK
    depth = lb_logits.shape[0]
    hps = HG_HEADS_PER_STEP
    wid = hps * LANES
    ngrp = HG_HEADS // hps
    tril, triu, lvf, lvb = _hgrn_constants(seq_len - (lp - c))
    col = lambda g: pl.BlockSpec((1, lp, wid), lambda i, h, g=g: (i, 0, g * ngrp + h))
    const = lambda shape: pl.BlockSpec(shape, lambda i, h: (0,) * len(shape))
    return pl.pallas_call(
        functools.partial(_hgrn_kernel, layer=layer, seq_len=seq_len),
        out_shape=jax.ShapeDtypeStruct((b, lp, HG_HEADS * HG_VDIM), BF16),
        grid=(b, ngrp),
        in_specs=[
            col(0), col(1), col(2), col(3), col(4),
            pl.BlockSpec((depth, 2, wid), lambda i, h: (0, 0, h)),
            const((1, HG_VDIM)),
            const((2, c, c)), const((2, c, c)), const((c, c)), const((c, c)),
        ],
        out_specs=pl.BlockSpec((1, lp, wid), lambda i, h: (i, 0, h)),
        scratch_shapes=[
            pltpu.VMEM((lp, wid), F32), pltpu.VMEM((lp, wid), F32),
            pltpu.VMEM((hps, HG_VDIM, HG_KDIM), F32), pltpu.VMEM((hps, HG_VDIM, HG_KDIM), F32),
            pltpu.VMEM((hps, c, HG_KDIM), F32), pltpu.VMEM((hps, c, HG_KDIM), F32),
        ],
        compiler_params=pltpu.CompilerParams(
            dimension_semantics=("parallel", "parallel"), vmem_limit_bytes=VMEM_LIMIT),
        name="hgrn2",
    )(p3d, p3d, p3d, p3d, p3d, lb_logits, gain.reshape(1, HG_VDIM), tril, triu, lvf, lvb)


def _rope_tables(positions):
    half = AT_HEAD_DIM // 2
    inv = ROPE_THETA ** (-jnp.arange(0, AT_HEAD_DIM, 2, dtype=F32) / AT_HEAD_DIM)
    ang = positions.astype(F32)[:, None] * inv[None, :]
    cos, sin = jnp.cos(ang), jnp.sin(ang)
    reps = LANES // half
    cos_t = jnp.tile(cos, (1, reps))
    sign = jnp.where((jnp.arange(LANES) % AT_HEAD_DIM) < half, -1.0, 1.0).astype(F32)
    sin_t = jnp.tile(sin, (1, reps)) * sign[None, :]
    return cos_t, sin_t


def _rope(t, cos_t, sin_t):
    half = AT_HEAD_DIM // 2
    lane = lax.broadcasted_iota(jnp.int32, t.shape, 1)
    first = (lane % AT_HEAD_DIM) < half
    partner = jnp.where(first, pltpu.roll(t, LANES - half, axis=1), pltpu.roll(t, half, axis=1))
    return t * cos_t + partner * sin_t


AT_TQ_CAP = 704


def _attn_tile(seq_len):
    best = None
    for t in range(16, min(seq_len, AT_TQ_CAP) + 1, 16):
        if seq_len % t == 0:
            best = t
    assert best is not None, seq_len
    return best


_BIAS_LANE = (AT_HEAD_DIM, 0)


def _attn_kernel(q_ref, k_ref, v_ref, gate_ref, cos_ref, sin_ref, lamv_ref, gain_ref, o_ref,
                 k_s, vaug_s, s_s, m_s, *, tq, lam_init, seq_len):
    lp = k_ref.shape[1]
    nq = seq_len // tq

    tk = _pick_tile(lp, AT_TQ_CAP, LANES)
    for n in range(lp // tk):
        rows = pl.ds(n * tk, tk)
        kr = _rope(k_ref[0, rows, :].astype(F32), cos_ref[rows, :], sin_ref[rows, :])
        lane = lax.broadcasted_iota(jnp.int32, (tk, LANES), 1)
        pos = n * tk + lax.broadcasted_iota(jnp.int32, (tk, LANES), 0)
        bias = jnp.where(pos < seq_len, 0.0, KEY_MASK)
        for c in range(2):
            in_half = (lane < AT_HEAD_DIM) if c == 0 else (lane >= AT_HEAD_DIM)
            kc = jnp.where(in_half, kr, jnp.where(lane == _BIAS_LANE[c], bias, 0.0))
            k_s[c, rows, :] = kc.astype(BF16)
        vaug_s[rows, 0:LANES] = v_ref[0, rows, :]
        vaug_s[rows, LANES:2 * LANES] = jnp.ones((tk, LANES), BF16)

    lv = lamv_ref[...]
    lam = (jnp.exp(jnp.sum(lv[0:1] * lv[1:2], axis=-1, keepdims=True))
           - jnp.exp(jnp.sum(lv[2:3] * lv[3:4], axis=-1, keepdims=True)) + lam_init)

    def scores(j, slot):
        rows = pl.ds(j * tq, tq)
        q = _rope(q_ref[0, rows, :].astype(F32), cos_ref[rows, :], sin_ref[rows, :])
        q = q * (AT_HEAD_DIM ** -0.5 * LOG2E)
        lane = lax.broadcasted_iota(jnp.int32, (tq, LANES), 1)
        for c in range(2):
            in_half = (lane < AT_HEAD_DIM) if c == 0 else (lane >= AT_HEAD_DIM)
            qc = jnp.where(in_half, q, jnp.where(lane == _BIAS_LANE[c], 1.0, 0.0)).astype(BF16)
            s = _dot_nt(qc, k_s[c])
            s_s[slot, c] = s
            m_s[slot, c] = jnp.max(s, axis=-1, keepdims=True)

    def outputs(j, slot):
        rows = pl.ds(j * tq, tq)
        outs = []
        for c in range(2):
            p = jnp.exp2(s_s[slot, c] - m_s[slot, c]).astype(BF16)
            ol = _dot(p, vaug_s[...])
            outs.append(ol[:, 0:LANES] / ol[:, LANES:2 * LANES])
        o = outs[0] - lam * outs[1]
        ms = jnp.mean(o * o, axis=-1, keepdims=True)
        y = (o * lax.rsqrt(ms + NORM_EPS) * gain_ref[...] * (1.0 - lam_init)
             * _silu(gate_ref[0, rows, :].astype(F32)))
        o_ref[0, rows, :] = y.astype(o_ref.dtype)

    scores(0, 0)
    for j in range(nq):
        if j + 1 < nq:
            scores(j + 1, (j + 1) % 2)
        outputs(j, j % 2)
    if seq_len < lp:
        o_ref[0, seq_len:lp, :] = jnp.zeros((lp - seq_len, LANES), o_ref.dtype)


def _attn(p3d, lam_vecs, gain, layer, seq_len, cos_t, sin_t):
    b, lp, _ = p3d.shape
    tq = _attn_tile(seq_len)
    lam_init = 0.8 - 0.6 * math.exp(-0.3 * layer)
    g0 = 5 * AT_HEADS
    col = lambda g: pl.BlockSpec((1, lp, LANES), lambda i, h, g=g: (i, 0, g0 + g * AT_HEADS + h))
    const = lambda shape: pl.BlockSpec(shape, lambda i, h: (0,) * len(shape))
    return pl.pallas_call(
        functools.partial(_attn_kernel, tq=tq, lam_init=lam_init, seq_len=seq_len),
        out_shape=jax.ShapeDtypeStruct((b, lp, AT_HEADS * 2 * AT_HEAD_DIM), BF16),
        grid=(b, AT_HEADS),
        in_specs=[
            col(0), col(1), col(2), col(3),
            const((lp, LANES)), const((lp, LANES)),
            const((4, AT_HEAD_DIM)), const((1, LANES)),
        ],
        out_specs=pl.BlockSpec((1, lp, LANES), lambda i, h: (i, 0, h)),
        scratch_shapes=[
            pltpu.VMEM((2, lp, LANES), BF16),
            pltpu.VMEM((lp, 2 * LANES), BF16),
            pltpu.VMEM((2, 2, tq, lp), F32),
            pltpu.VMEM((2, 2, tq, 1), F32),
        ],
        compiler_params=pltpu.CompilerParams(
            dimension_semantics=("parallel", "parallel"), vmem_limit_bytes=VMEM_LIMIT),
        name="diff_attn",
    )(p3d, p3d, p3d, p3d, cos_t, sin_t, lam_vecs, gain.reshape(1, LANES))


def _outproj_kernel(oh_ref, oa_ref, w_ref, h_ref, fg_ref, out_ref, *, final):
    wh = oh_ref.shape[1]
    acc = h_ref[...] + _dot(oh_ref[...], w_ref[0:wh, :]) + _dot(oa_ref[...], w_ref[wh:, :])
    if final:
        ms = jnp.mean(acc * acc, axis=-1, keepdims=True)
        acc = acc * lax.rsqrt(ms + NORM_EPS) * fg_ref[...]
    out_ref[...] = acc


def _outproj(o_h, o_a, w_bf16, layer, h3d, final_g, rows, final):
    b, _, d = h3d.shape
    wh, wa = o_h.shape[2], o_a.shape[2]
    tm = _pick_tile(rows, OUTPROJ_TM_CAP, 16)
    row_block = lambda w: pl.BlockSpec((None, tm, w), lambda i, j: (i, j, 0))
    return pl.pallas_call(
        functools.partial(_outproj_kernel, final=final),
        out_shape=jax.ShapeDtypeStruct((b, rows, d), F32),
        grid=(b, rows // tm),
        in_specs=[
            row_block(wh), row_block(wa),
            pl.BlockSpec((None, wh + wa, d), lambda i, j: (layer, 0, 0)),
            row_block(d),
            pl.BlockSpec((1, d), lambda i, j: (0, 0)),
        ],
        out_specs=row_block(d),
        compiler_params=pltpu.CompilerParams(
            dimension_semantics=("parallel", "parallel"), vmem_limit_bytes=VMEM_LIMIT),
        name="outproj",
    )(o_h, o_a, w_bf16, h3d, final_g.reshape(1, d))


def kernel(x, meta_tokens, norm_g, w_in, hg_lb_logits, hg_norm_g, diff_lambda, diff_subln_g,
           w_out, final_norm_g):
    bsz, seq, d = x.shape
    depth = w_in.shape[0]
    seq_len = N_META + seq
    lp = seq + HG_CHUNK
    assert seq % HG_CHUNK == 0 and N_META <= HG_CHUNK
    assert w_in.shape[2] == N_GROUPS * HG_HEADS * LANES

    meta = jnp.broadcast_to(meta_tokens[None].astype(x.dtype), (bsz, N_META, d))
    pad = jnp.zeros((bsz, lp - seq_len, d), x.dtype)
    h = jnp.concatenate([x, meta, pad], axis=1)
    positions = jnp.concatenate([jnp.arange(N_META, seq_len), jnp.arange(N_META),
                                 jnp.zeros((lp - seq_len,), jnp.int32)])
    cos_t, sin_t = _rope_tables(positions)
    w_in_bf = w_in.astype(BF16)
    w_out_bf = w_out.astype(BF16)

    for l in range(depth):
        final = l == depth - 1
        p = _norm_inproj(h.reshape(bsz * lp, d), norm_g[l], w_in_bf, l).reshape(bsz, lp, -1)
        o_h = _hgrn(p, hg_lb_logits, hg_norm_g[l], l, seq_len)
        o_a = _attn(p, diff_lambda[l], diff_subln_g[l], l, seq_len, cos_t, sin_t)
        h = _outproj(o_h, o_a, w_out_bf, l, h, final_norm_g, seq if final else lp, final)
    return h
```

```python
import functools
import math

import numpy as np
import jax
import jax.numpy as jnp
from jax import lax
from jax.experimental import pallas as pl
from jax.experimental.pallas import tpu as pltpu

F32 = jnp.float32
BF16 = jnp.bfloat16

N_META = 16
HG_HEADS = 8
HG_KDIM = 128
HG_VDIM = 128
AT_HEADS = 8
AT_HEAD_DIM = 64
ROPE_THETA = 10000.0
NORM_EPS = 1e-6
F_FLOOR = 1e-30
N_GROUPS = 9
LOG2E = math.log2(math.e)

LANES = 128
SUBLANES = 8
VMEM_LIMIT = 56 * 1024 * 1024

HG_CHUNK = 128
HG_LEVELS = (64, 32, 16, 8, 4, 2, 1)
HG_DIAG_LEVEL = len(HG_LEVELS)
HG_HEADS_PER_STEP = 4

KEY_MASK = -1e30
INPROJ_TN_CAP = 1024
OUTPROJ_TM_CAP = 576


def _pick_tile(n, cap, mult):
    best = None
    for t in range(mult, min(n, cap) + 1, mult):
        if n % t == 0:
            best = t
    assert best is not None, (n, cap, mult)
    return best


def _dot(a, b):
    return jnp.dot(a, b, preferred_element_type=F32)


def _dot_nt(a, b):
    return lax.dot_general(a, b, (((1,), (1,)), ((), ())), preferred_element_type=F32)


def _silu(g):
    return g * (1.0 / (1.0 + jnp.exp(-g)))


def _norm_inproj_kernel(body_ref, tail_ref, g_ref, w_ref, *rest, emit_h):
    if emit_h:
        body_col_ref, tail_col_ref, p_ref, h_ref, u_ref = rest
    else:
        p_ref, u_ref = rest
    nbody = body_ref.shape[0]

    @pl.when(pl.program_id(1) == 0)
    def _():
        for src, rows in ((body_ref, slice(0, nbody)), (tail_ref, slice(nbody, None))):
            x = src[...]
            ms = jnp.mean(x * x, axis=-1, keepdims=True)
            u_ref[rows, :] = (x * lax.rsqrt(ms + NORM_EPS) * g_ref[...]).astype(BF16)

    if emit_h:
        h_ref[0:nbody, :] = body_col_ref[...]
        h_ref[nbody:, :] = tail_col_ref[...]
    p_ref[...] = _dot(u_ref[...], w_ref[...]).astype(BF16)


def _norm_inproj(body, tail, g, w_bf16, layer, emit_h):
    b, d = body.shape[0], body.shape[2]
    ntail = HG_CHUNK
    nbody = body.shape[1] if tail.ndim == 2 else body.shape[1] - ntail
    lp = nbody + ntail
    n = w_bf16.shape[2]
    tn = _pick_tile(n, INPROJ_TN_CAP, LANES)
    ncol = d // LANES
    if tail.ndim == 2:
        tail_spec = pl.BlockSpec((ntail, d), lambda i, j: (0, 0))
    else:
        tail_spec = pl.BlockSpec((None, ntail, d), lambda i, j: (i, nbody // ntail, 0))
    in_specs = [
        pl.BlockSpec((None, nbody, d), lambda i, j: (i, 0, 0)),
        tail_spec,
        pl.BlockSpec((1, d), lambda i, j: (0, 0)),
        pl.BlockSpec((None, d, tn), lambda i, j: (layer, 0, j)),
    ]
    operands = [body, tail, g.reshape(1, d), w_bf16]
    out_shape = [jax.ShapeDtypeStruct((b, lp, n), BF16)]
    out_specs = [pl.BlockSpec((None, lp, tn), lambda i, j: (i, 0, j))]
    if emit_h:
        assert tail.ndim == 2 and n // tn >= ncol
        col = lambda j: jnp.minimum(j, ncol - 1)
        in_specs += [pl.BlockSpec((None, nbody, LANES), lambda i, j: (i, 0, col(j))),
                     pl.BlockSpec((ntail, LANES), lambda i, j: (0, col(j)))]
        operands += [body, tail]
        out_shape.append(jax.ShapeDtypeStruct((b, lp, d), F32))
        out_specs.append(pl.BlockSpec((None, lp, LANES), lambda i, j: (i, 0, col(j))))
    return pl.pallas_call(
        functools.partial(_norm_inproj_kernel, emit_h=emit_h),
        out_shape=out_shape,
        grid=(b, n // tn),
        in_specs=in_specs,
        out_specs=out_specs,
        scratch_shapes=[pltpu.VMEM((lp, d), BF16)],
        compiler_params=pltpu.CompilerParams(
            dimension_semantics=("parallel", "arbitrary"), vmem_limit_bytes=VMEM_LIMIT),
        name="norm_inproj",
    )(*operands)


def _hgrn_constants(last_valid):
    c = HG_CHUNK
    t = np.arange(c)[:, None]
    s = np.arange(c)[None, :]
    tril = (s <= t).astype(np.float32)
    real = (np.arange(c) < last_valid).astype(np.float32)[None, :]
    tril = np.stack([tril, tril * real])
    triu = np.stack([tril[0].T, tril[0].T * real])
    x = t ^ s
    top = np.where(x > 0, np.floor(np.log2(np.maximum(x, 1))).astype(np.int64), 0)
    half = 1 << top
    level = np.full((c, c), -1, np.int32)
    for li, hh in enumerate(HG_LEVELS):
        level = np.where((t > s) & (half == hh), li, level)
    level = np.where(t == s, HG_DIAG_LEVEL, level).astype(np.int32)
    return (jnp.asarray(tril, BF16), jnp.asarray(triu, BF16),
            jnp.asarray(level), jnp.asarray(level.T))


def _split2(x):
    hi = x.astype(BF16)
    return hi, (x - hi.astype(F32)).astype(BF16)


def _level_operands(q, kin, cum, cum_ref, half, rev):
    c = HG_CHUNK
    blk = 2 * half
    off = half if rev else half - 1
    zeros = jnp.zeros((half, LANES), F32)
    qt, kt = [], []
    for m in range(c // blk):
        ref = cum_ref[pl.ds(m * blk + off, 1), :]
        lo = slice(m * blk, m * blk + half)
        hi = slice(m * blk + half, (m + 1) * blk)
        if rev:
            qt += [q[lo] * jnp.exp2(cum[lo] - ref), zeros]
            kt += [zeros, kin[hi] * jnp.exp2(ref - cum[hi])]
        else:
            kt += [kin[lo] * jnp.exp2(ref - cum[lo]), zeros]
            qt += [zeros, q[hi] * jnp.exp2(cum[hi] - ref)]
    return jnp.concatenate(qt, axis=0), jnp.concatenate(kt, axis=0)


def _level_exponent(cum, cum_ref, half, rev):
    c = HG_CHUNK
    blk = 2 * half
    off = half if rev else half - 1
    parts = []
    assert half < SUBLANES
    sub = lax.broadcasted_iota(jnp.int32, (SUBLANES, LANES), 0)
    for g in range(c // SUBLANES):
        base = g * SUBLANES
        ref = jnp.broadcast_to(cum_ref[pl.ds(base + off, 1), :], (SUBLANES, LANES))
        if blk < SUBLANES:
            hi = jnp.broadcast_to(cum_ref[pl.ds(base + blk + off, 1), :], (SUBLANES, LANES))
            ref = jnp.where(sub < blk, ref, hi)
        parts.append(-jnp.abs(cum[base:base + SUBLANES] - ref))
    return jnp.concatenate(parts, axis=0)


def _hgrn_chunks(chains):
    c = HG_CHUNK
    row = lax.broadcasted_iota(jnp.int32, (c, LANES), 0)
    for ch in chains:
        sig = 1.0 / (1.0 + jnp.exp(-ch["z"]))
        f = ch["lb"] + (1.0 - ch["lb"]) * sig
        ch["kin"] = 1.0 - f
        ch["f"] = jnp.maximum(f, F_FLOOR)
        ch["parts"] = _split2(jnp.log(ch["f"]))
    for ch in chains:
        hi, lo = ch["parts"]
        cum = (_dot(ch["tri"], hi) + _dot(ch["tri"], lo)) * LOG2E
        ch["cum_ref"][...] = cum
        ch["cum"] = cum
        ch["tot"] = cum[0:1] if ch["rev"] else cum[c - 1:c]
    for ch in chains:
        cum, tot = ch["cum"], ch["tot"]
        q_dec = ch["q"] * jnp.exp2(cum)
        k_dec = ch["kin"] * jnp.exp2(tot - cum)
        st = ch["st_ref"][...]
        ch["o"] = _dot_nt(q_dec.astype(BF16), st.astype(BF16))
        ch["v_bf"] = ch["v"].astype(BF16)
        ch["st_ref"][...] = st * jnp.exp2(tot) + _dot(ch["v"].T.astype(BF16), k_dec.astype(BF16))
        ch["att"] = jnp.where(ch["level"] == HG_DIAG_LEVEL,
                              _dot_nt(ch["q"].astype(BF16), ch["kin"].astype(BF16)), 0.0)
    for li, half in enumerate(HG_LEVELS):
        for ch in chains:
            if half >= SUBLANES:
                qt, kt = _level_operands(ch["q"], ch["kin"], ch["cum"], ch["cum_ref"], half, ch["rev"])
            else:
                if half == 1:
                    q_side = (row % 2 == 0) if ch["rev"] else (row % 2 == 1)
                    ex = jnp.where(q_side, ch["f"], 1.0)
                else:
                    ex = jnp.exp2(_level_exponent(ch["cum"], ch["cum_ref"], half, ch["rev"]))
                qt, kt = ch["q"] * ex, ch["kin"] * ex
            a = _dot_nt(qt.astype(BF16), kt.astype(BF16))
            ch["att"] = jnp.where(ch["level"] == li, a, ch["att"])
    return [ch["o"] + _dot(ch["att"].astype(BF16), ch["v_bf"]) for ch in chains]


def _hgrn_kernel(q_ref, zf_ref, zb_ref, v_ref, gate_ref, lbl_ref, gain_ref,
                 tril_ref, triu_ref, lvf_ref, lvb_ref, o_ref,
                 of_s, ob_s, stf_s, stb_s, cumf_s, cumb_s, *, layer, seq_len):
    c = HG_CHUNK
    lp = q_ref.shape[1]
    nc = lp // c
    depth = lbl_ref.shape[0]
    nheads = q_ref.shape[2] // LANES

    logits = [lbl_ref[d] for d in range(depth)]
    mx = functools.reduce(jnp.maximum, logits)
    es = [jnp.exp(x - mx) for x in logits]
    den = functools.reduce(lambda a, b: a + b, es)
    num = jnp.zeros_like(den)
    for d in range(1, layer + 1):
        num = num + es[d]
    lbs = num / den

    stf_s[...] = jnp.zeros_like(stf_s)
    stb_s[...] = jnp.zeros_like(stb_s)

    def body(n, carry):
        cf = lax.rem(n + nc - 1, nc)
        cb = lax.rem(2 * nc - 2 - n, nc)
        rf = pl.ds(pl.multiple_of(cf * c, c), c)
        rb = pl.ds(pl.multiple_of(cb * c, c), c)
        tri_f = tril_ref[(cf == nc - 1).astype(jnp.int32)]
        tri_b = triu_ref[(cb == nc - 1).astype(jnp.int32)]
        chains = []
        for hh in range(nheads):
            ln = pl.ds(hh * LANES, LANES)
            lanes = slice(hh * LANES, (hh + 1) * LANES)
            chains.append(dict(
                q=q_ref[0, rf, ln].astype(F32), z=zf_ref[0, rf, ln].astype(F32), v=v_ref[0, rf, ln].astype(F32),
                lb=lbs[0:1, lanes], st_ref=stf_s.at[hh], cum_ref=cumf_s.at[hh],
                tri=tri_f, level=lvf_ref[...], rev=False, out=(of_s, rf, ln)))
            chains.append(dict(
                q=q_ref[0, rb, ln].astype(F32), z=zb_ref[0, rb, ln].astype(F32), v=v_ref[0, rb, ln].astype(F32),
                lb=lbs[1:2, lanes], st_ref=stb_s.at[hh], cum_ref=cumb_s.at[hh],
                tri=tri_b, level=lvb_ref[...], rev=True, out=(ob_s, rb, ln)))
        for ch, o in zip(chains, _hgrn_chunks(chains)):
            dst, rows, ln = ch["out"]
            dst[rows, ln] = o
        return carry

    lax.fori_loop(0, nc, body, 0, unroll=2)

    def epilogue(n, carry):
        rows = pl.ds(pl.multiple_of(n * c, c), c)
        pos = n * c + lax.broadcasted_iota(jnp.int32, (c, LANES), 0)
        for hh in range(nheads):
            ln = pl.ds(hh * LANES, LANES)
            o = of_s[rows, ln] + ob_s[rows, ln]
            ms = jnp.mean(o * o, axis=-1, keepdims=True)
            y = o * lax.rsqrt(ms + NORM_EPS) * gain_ref[...] * _silu(gate_ref[0, rows, ln].astype(F32))
            o_ref[0, rows, ln] = jnp.where(pos < seq_len, y, 0.0).astype(o_ref.dtype)
        return carry

    lax.fori_loop(0, nc, epilogue, 0)


def _hgrn(p3d, lb_logits, gain, layer, seq_len):
    b, lp, _ = p3d.shape
    c = HG_CHUNK
    depth = lb_logits.shape[0]
    hps = HG_HEADS_PER_STEP
    wid = hps * LANES
    ngrp = HG_HEADS // hps
    tril, triu, lvf, lvb = _hgrn_constants(seq_len - (lp - c))
    col = lambda g: pl.BlockSpec((1, lp, wid), lambda i, h, g=g: (i, 0, g * ngrp + h))
    const = lambda shape: pl.BlockSpec(shape, lambda i, h: (0,) * len(shape))
    return pl.pallas_call(
        functools.partial(_hgrn_kernel, layer=layer, seq_len=seq_len),
        out_shape=jax.ShapeDtypeStruct((b, lp, HG_HEADS * HG_VDIM), BF16),
        grid=(b, ngrp),
        in_specs=[
            col(0), col(1), col(2), col(3), col(4),
            pl.BlockSpec((depth, 2, wid), lambda i, h: (0, 0, h)),
            const((1, HG_VDIM)),
            const((2, c, c)), const((2, c, c)), const((c, c)), const((c, c)),
        ],
        out_specs=pl.BlockSpec((1, lp, wid), lambda i, h: (i, 0, h)),
        scratch_shapes=[
            pltpu.VMEM((lp, wid), F32), pltpu.VMEM((lp, wid), F32),
            pltpu.VMEM((hps, HG_VDIM, HG_KDIM), F32), pltpu.VMEM((hps, HG_VDIM, HG_KDIM), F32),
            pltpu.VMEM((hps, c, HG_KDIM), F32), pltpu.VMEM((hps, c, HG_KDIM), F32),
        ],
        compiler_params=pltpu.CompilerParams(
            dimension_semantics=("parallel", "parallel"), vmem_limit_bytes=VMEM_LIMIT),
        name="hgrn2",
    )(p3d, p3d, p3d, p3d, p3d, lb_logits, gain.reshape(1, HG_VDIM), tril, triu, lvf, lvb)


def _rope_tables(positions):
    half = AT_HEAD_DIM // 2
    inv = ROPE_THETA ** (-jnp.arange(0, AT_HEAD_DIM, 2, dtype=F32) / AT_HEAD_DIM)
    ang = positions.astype(F32)[:, None] * inv[None, :]
    cos, sin = jnp.cos(ang), jnp.sin(ang)
    reps = LANES // half
    cos_t = jnp.tile(cos, (1, reps))
    sign = jnp.where((jnp.arange(LANES) % AT_HEAD_DIM) < half, -1.0, 1.0).astype(F32)
    sin_t = jnp.tile(sin, (1, reps)) * sign[None, :]
    return cos_t, sin_t


def _rope(t, cos_t, sin_t):
    half = AT_HEAD_DIM // 2
    lane = lax.broadcasted_iota(jnp.int32, t.shape, 1)
    first = (lane % AT_HEAD_DIM) < half
    partner = jnp.where(first, pltpu.roll(t, LANES - half, axis=1), pltpu.roll(t, half, axis=1))
    return t * cos_t + partner * sin_t


AT_TQ_CAP = 704


def _attn_tile(seq_len):
    best = None
    for t in range(16, min(seq_len, AT_TQ_CAP) + 1, 16):
        if seq_len % t == 0:
            best = t
    assert best is not None, seq_len
    return best


_BIAS_LANE = (AT_HEAD_DIM, 0)


def _attn_kernel(q_ref, k_ref, v_ref, gate_ref, cos_ref, sin_ref, lamv_ref, gain_ref, o_ref,
                 k_s, vaug_s, s_s, m_s, *, tq, lam_init, seq_len):
    lp = k_ref.shape[1]
    nq = seq_len // tq

    tk = _pick_tile(lp, AT_TQ_CAP, LANES)
    for n in range(lp // tk):
        rows = pl.ds(n * tk, tk)
        kr = _rope(k_ref[0, rows, :].astype(F32), cos_ref[rows, :], sin_ref[rows, :])
        lane = lax.broadcasted_iota(jnp.int32, (tk, LANES), 1)
        pos = n * tk + lax.broadcasted_iota(jnp.int32, (tk, LANES), 0)
        bias = jnp.where(pos < seq_len, 0.0, KEY_MASK)
        for c in range(2):
            in_half = (lane < AT_HEAD_DIM) if c == 0 else (lane >= AT_HEAD_DIM)
            kc = jnp.where(in_half, kr, jnp.where(lane == _BIAS_LANE[c], bias, 0.0))
            k_s[c, rows, :] = kc.astype(BF16)
        vaug_s[rows, 0:LANES] = v_ref[0, rows, :]
        vaug_s[rows, LANES:2 * LANES] = jnp.ones((tk, LANES), BF16)

    lv = lamv_ref[...]
    lam = (jnp.exp(jnp.sum(lv[0:1] * lv[1:2], axis=-1, keepdims=True))
           - jnp.exp(jnp.sum(lv[2:3] * lv[3:4], axis=-1, keepdims=True)) + lam_init)

    def scores(j, slot):
        rows = pl.ds(j * tq, tq)
        q = _rope(q_ref[0, rows, :].astype(F32), cos_ref[rows, :], sin_ref[rows, :])
        q = q * (AT_HEAD_DIM ** -0.5 * LOG2E)
        lane = lax.broadcasted_iota(jnp.int32, (tq, LANES), 1)
        for c in range(2):
            in_half = (lane < AT_HEAD_DIM) if c == 0 else (lane >= AT_HEAD_DIM)
            qc = jnp.where(in_half, q, jnp.where(lane == _BIAS_LANE[c], 1.0, 0.0)).astype(BF16)
            s = _dot_nt(qc, k_s[c])
            s_s[slot, c] = s
            m_s[slot, c] = jnp.max(s, axis=-1, keepdims=True)

    def outputs(j, slot):
        rows = pl.ds(j * tq, tq)
        outs = []
        for c in range(2):
            p = jnp.exp2(s_s[slot, c] - m_s[slot, c]).astype(BF16)
            ol = _dot(p, vaug_s[...])
            outs.append(ol[:, 0:LANES] / ol[:, LANES:2 * LANES])
        o = outs[0] - lam * outs[1]
        ms = jnp.mean(o * o, axis=-1, keepdims=True)
        y = (o * lax.rsqrt(ms + NORM_EPS) * gain_ref[...] * (1.0 - lam_init)
             * _silu(gate_ref[0, rows, :].astype(F32)))
        o_ref[0, rows, :] = y.astype(o_ref.dtype)

    scores(0, 0)
    for j in range(nq):
        if j + 1 < nq:
            scores(j + 1, (j + 1) % 2)
        outputs(j, j % 2)
    if seq_len < lp:
        o_ref[0, seq_len:lp, :] = jnp.zeros((lp - seq_len, LANES), o_ref.dtype)


def _attn(p3d, lam_vecs, gain, layer, seq_len, cos_t, sin_t):
    b, lp, _ = p3d.shape
    tq = _attn_tile(seq_len)
    lam_init = 0.8 - 0.6 * math.exp(-0.3 * layer)
    g0 = 5 * AT_HEADS
    col = lambda g: pl.BlockSpec((1, lp, LANES), lambda i, h, g=g: (i, 0, g0 + g * AT_HEADS + h))
    const = lambda shape: pl.BlockSpec(shape, lambda i, h: (0,) * len(shape))
    return pl.pallas_call(
        functools.partial(_attn_kernel, tq=tq, lam_init=lam_init, seq_len=seq_len),
        out_shape=jax.ShapeDtypeStruct((b, lp, AT_HEADS * 2 * AT_HEAD_DIM), BF16),
        grid=(b, AT_HEADS),
        in_specs=[
            col(0), col(1), col(2), col(3),
            const((lp, LANES)), const((lp, LANES)),
            const((4, AT_HEAD_DIM)), const((1, LANES)),
        ],
        out_specs=pl.BlockSpec((1, lp, LANES), lambda i, h: (i, 0, h)),
        scratch_shapes=[
            pltpu.VMEM((2, lp, LANES), BF16),
            pltpu.VMEM((lp, 2 * LANES), BF16),
            pltpu.VMEM((2, 2, tq, lp), F32),
            pltpu.VMEM((2, 2, tq, 1), F32),
        ],
        compiler_params=pltpu.CompilerParams(
            dimension_semantics=("parallel", "parallel"), vmem_limit_bytes=VMEM_LIMIT),
        name="diff_attn",
    )(p3d, p3d, p3d, p3d, cos_t, sin_t, lam_vecs, gain.reshape(1, LANES))


def _outproj_kernel(oh_ref, oa_ref, w_ref, h_ref, fg_ref, out_ref, *, final):
    wh = oh_ref.shape[1]
    acc = h_ref[...] + _dot(oh_ref[...], w_ref[0:wh, :]) + _dot(oa_ref[...], w_ref[wh:, :])
    if final:
        ms = jnp.mean(acc * acc, axis=-1, keepdims=True)
        acc = acc * lax.rsqrt(ms + NORM_EPS) * fg_ref[...]
    out_ref[...] = acc


def _outproj(o_h, o_a, w_bf16, layer, h3d, final_g, rows, final):
    b, _, d = h3d.shape
    wh, wa = o_h.shape[2], o_a.shape[2]
    tm = _pick_tile(rows, OUTPROJ_TM_CAP, 16)
    row_block = lambda w: pl.BlockSpec((None, tm, w), lambda i, j: (i, j, 0))
    return pl.pallas_call(
        functools.partial(_outproj_kernel, final=final),
        out_shape=jax.ShapeDtypeStruct((b, rows, d), F32),
        grid=(b, rows // tm),
        in_specs=[
            row_block(wh), row_block(wa),
            pl.BlockSpec((None, wh + wa, d), lambda i, j: (layer, 0, 0)),
            row_block(d),
            pl.BlockSpec((1, d), lambda i, j: (0, 0)),
        ],
        out_specs=row_block(d),
        compiler_params=pltpu.CompilerParams(
            dimension_semantics=("parallel", "parallel"), vmem_limit_bytes=VMEM_LIMIT),
        name="outproj",
    )(o_h, o_a, w_bf16, h3d, final_g.reshape(1, d))


def kernel(x, meta_tokens, norm_g, w_in, hg_lb_logits, hg_norm_g, diff_lambda, diff_subln_g,
           w_out, final_norm_g):
    bsz, seq, d = x.shape
    depth = w_in.shape[0]
    seq_len = N_META + seq
    lp = seq + HG_CHUNK
    assert seq % HG_CHUNK == 0 and N_META <= HG_CHUNK
    assert w_in.shape[2] == N_GROUPS * HG_HEADS * LANES

    tail = jnp.concatenate([meta_tokens.astype(x.dtype), jnp.zeros((lp - seq_len, d), x.dtype)])
    positions = jnp.concatenate([jnp.arange(N_META, seq_len), jnp.arange(N_META),
                                 jnp.zeros((lp - seq_len,), jnp.int32)])
    cos_t, sin_t = _rope_tables(positions)
    w_in_bf = w_in.astype(BF16)
    w_out_bf = w_out.astype(BF16)

    for l in range(depth):
        final = l == depth - 1
        if l == 0:
            p, h = _norm_inproj(x, tail, norm_g[l], w_in_bf, l, emit_h=True)
        else:
            p, = _norm_inproj(h, h, norm_g[l], w_in_bf, l, emit_h=False)
        o_h = _hgrn(p, hg_lb_logits, hg_norm_g[l], l, seq_len)
        o_a = _attn(p, diff_lambda[l], diff_subln_g[l], l, seq_len, cos_t, sin_t)
        h = _outproj(o_h, o_a, w_out_bf, l, h, final_norm_g, seq if final else lp, final)
    return h
```

```python
import functools
import math

import numpy as np
import jax
import jax.numpy as jnp
from jax import lax
from jax.experimental import pallas as pl
from jax.experimental.pallas import tpu as pltpu

F32 = jnp.float32
BF16 = jnp.bfloat16

N_META = 16
HG_HEADS = 8
HG_KDIM = 128
HG_VDIM = 128
AT_HEADS = 8
AT_HEAD_DIM = 64
ROPE_THETA = 10000.0
NORM_EPS = 1e-6
F_FLOOR = 1e-30
N_GROUPS = 9
LOG2E = math.log2(math.e)

LANES = 128
SUBLANES = 8
VMEM_LIMIT = 56 * 1024 * 1024

HG_CHUNK = 128
HG_LEVELS = (64, 32, 16, 8, 4, 2, 1)
HG_DIAG_LEVEL = len(HG_LEVELS)
HG_HEADS_PER_STEP = 4

KEY_MASK = -1e30
INPROJ_TN_CAP = 1024
OUTPROJ_TM_CAP = 576


def _pick_tile(n, cap, mult):
    best = None
    for t in range(mult, min(n, cap) + 1, mult):
        if n % t == 0:
            best = t
    assert best is not None, (n, cap, mult)
    return best


def _dot(a, b):
    return jnp.dot(a, b, preferred_element_type=F32)


def _dot_nt(a, b):
    return lax.dot_general(a, b, (((1,), (1,)), ((), ())), preferred_element_type=F32)


def _silu(g):
    return g * (1.0 / (1.0 + jnp.exp(-g)))


def _norm_inproj_kernel(body_ref, tail_ref, g_ref, w_ref, *rest, emit_h):
    if emit_h:
        body_col_ref, tail_col_ref, p_ref, h_ref, u_ref = rest
    else:
        p_ref, u_ref = rest
    nbody = body_ref.shape[0]

    @pl.when(pl.program_id(1) == 0)
    def _():
        for src, rows in ((body_ref, slice(0, nbody)), (tail_ref, slice(nbody, None))):
            x = src[...]
            ms = jnp.mean(x * x, axis=-1, keepdims=True)
            u_ref[rows, :] = (x * lax.rsqrt(ms + NORM_EPS) * g_ref[...]).astype(BF16)

    if emit_h:
        h_ref[0:nbody, :] = body_col_ref[...]
        h_ref[nbody:, :] = tail_col_ref[...]
    p_ref[...] = _dot(u_ref[...], w_ref[...].astype(BF16)).astype(BF16)


def _norm_inproj(body, tail, g, w, layer, emit_h):
    b, d = body.shape[0], body.shape[2]
    ntail = HG_CHUNK
    nbody = body.shape[1] if tail.ndim == 2 else body.shape[1] - ntail
    lp = nbody + ntail
    n = w.shape[2]
    tn = _pick_tile(n, INPROJ_TN_CAP, LANES)
    ncol = d // LANES
    if tail.ndim == 2:
        tail_spec = pl.BlockSpec((ntail, d), lambda i, j: (0, 0))
    else:
        tail_spec = pl.BlockSpec((None, ntail, d), lambda i, j: (i, nbody // ntail, 0))
    in_specs = [
        pl.BlockSpec((None, nbody, d), lambda i, j: (i, 0, 0)),
        tail_spec,
        pl.BlockSpec((1, d), lambda i, j: (0, 0)),
        pl.BlockSpec((None, d, tn), lambda i, j: (layer, 0, j)),
    ]
    operands = [body, tail, g.reshape(1, d), w]
    out_shape = [jax.ShapeDtypeStruct((b, lp, n), BF16)]
    out_specs = [pl.BlockSpec((None, lp, tn), lambda i, j: (i, 0, j))]
    if emit_h:
        assert tail.ndim == 2 and n // tn >= ncol
        col = lambda j: jnp.minimum(j, ncol - 1)
        in_specs += [pl.BlockSpec((None, nbody, LANES), lambda i, j: (i, 0, col(j))),
                     pl.BlockSpec((ntail, LANES), lambda i, j: (0, col(j)))]
        operands += [body, tail]
        out_shape.append(jax.ShapeDtypeStruct((b, lp, d), F32))
        out_specs.append(pl.BlockSpec((None, lp, LANES), lambda i, j: (i, 0, col(j))))
    return pl.pallas_call(
        functools.partial(_norm_inproj_kernel, emit_h=emit_h),
        out_shape=out_shape,
        grid=(b, n // tn),
        in_specs=in_specs,
        out_specs=out_specs,
        scratch_shapes=[pltpu.VMEM((lp, d), BF16)],
        compiler_params=pltpu.CompilerParams(
            dimension_semantics=("parallel", "arbitrary"), vmem_limit_bytes=VMEM_LIMIT),
        name="norm_inproj",
    )(*operands)


def _hgrn_constants(last_valid):
    c = HG_CHUNK
    t = np.arange(c)[:, None]
    s = np.arange(c)[None, :]
    tril = (s <= t).astype(np.float32)
    real = (np.arange(c) < last_valid).astype(np.float32)[None, :]
    tril = np.stack([tril, tril * real])
    triu = np.stack([tril[0].T, tril[0].T * real])
    x = t ^ s
    top = np.where(x > 0, np.floor(np.log2(np.maximum(x, 1))).astype(np.int64), 0)
    half = 1 << top
    level = np.full((c, c), -1, np.int32)
    for li, hh in enumerate(HG_LEVELS):
        level = np.where((t > s) & (half == hh), li, level)
    level = np.where(t == s, HG_DIAG_LEVEL, level).astype(np.int32)
    return (jnp.asarray(tril, BF16), jnp.asarray(triu, BF16),
            jnp.asarray(level), jnp.asarray(level.T))


def _split2(x):
    hi = x.astype(BF16)
    return hi, (x - hi.astype(F32)).astype(BF16)


def _level_operands(q, kin, cum, cum_ref, half, rev):
    c = HG_CHUNK
    blk = 2 * half
    off = half if rev else half - 1
    zeros = jnp.zeros((half, LANES), F32)
    qt, kt = [], []
    for m in range(c // blk):
        ref = cum_ref[pl.ds(m * blk + off, 1), :]
        lo = slice(m * blk, m * blk + half)
        hi = slice(m * blk + half, (m + 1) * blk)
        if rev:
            qt += [q[lo] * jnp.exp2(cum[lo] - ref), zeros]
            kt += [zeros, kin[hi] * jnp.exp2(ref - cum[hi])]
        else:
            kt += [kin[lo] * jnp.exp2(ref - cum[lo]), zeros]
            qt += [zeros, q[hi] * jnp.exp2(cum[hi] - ref)]
    return jnp.concatenate(qt, axis=0), jnp.concatenate(kt, axis=0)


def _level_exponent(cum, cum_ref, half, rev):
    c = HG_CHUNK
    blk = 2 * half
    off = half if rev else half - 1
    parts = []
    assert half < SUBLANES
    sub = lax.broadcasted_iota(jnp.int32, (SUBLANES, LANES), 0)
    for g in range(c // SUBLANES):
        base = g * SUBLANES
        ref = jnp.broadcast_to(cum_ref[pl.ds(base + off, 1), :], (SUBLANES, LANES))
        if blk < SUBLANES:
            hi = jnp.broadcast_to(cum_ref[pl.ds(base + blk + off, 1), :], (SUBLANES, LANES))
            ref = jnp.where(sub < blk, ref, hi)
        parts.append(-jnp.abs(cum[base:base + SUBLANES] - ref))
    return jnp.concatenate(parts, axis=0)


def _hgrn_chunks(chains):
    c = HG_CHUNK
    row = lax.broadcasted_iota(jnp.int32, (c, LANES), 0)
    for ch in chains:
        sig = 1.0 / (1.0 + jnp.exp(-ch["z"]))
        f = ch["lb"] + (1.0 - ch["lb"]) * sig
        ch["kin"] = 1.0 - f
        ch["f"] = jnp.maximum(f, F_FLOOR)
        ch["parts"] = _split2(jnp.log(ch["f"]))
    for ch in chains:
        hi, lo = ch["parts"]
        cum = (_dot(ch["tri"], hi) + _dot(ch["tri"], lo)) * LOG2E
        ch["cum_ref"][...] = cum
        ch["cum"] = cum
        ch["tot"] = cum[0:1] if ch["rev"] else cum[c - 1:c]
    for ch in chains:
        cum, tot = ch["cum"], ch["tot"]
        q_dec = ch["q"] * jnp.exp2(cum)
        k_dec = ch["kin"] * jnp.exp2(tot - cum)
        st = ch["st_ref"][...]
        ch["o"] = _dot_nt(q_dec.astype(BF16), st.astype(BF16))
        ch["v_bf"] = ch["v"].astype(BF16)
        ch["st_ref"][...] = st * jnp.exp2(tot) + _dot(ch["v"].T.astype(BF16), k_dec.astype(BF16))
        ch["att"] = jnp.where(ch["level"] == HG_DIAG_LEVEL,
                              _dot_nt(ch["q"].astype(BF16), ch["kin"].astype(BF16)), 0.0)
    for li, half in enumerate(HG_LEVELS):
        for ch in chains:
            if half >= SUBLANES:
                qt, kt = _level_operands(ch["q"], ch["kin"], ch["cum"], ch["cum_ref"], half, ch["rev"])
            else:
                if half == 1:
                    q_side = (row % 2 == 0) if ch["rev"] else (row % 2 == 1)
                    ex = jnp.where(q_side, ch["f"], 1.0)
                else:
                    ex = jnp.exp2(_level_exponent(ch["cum"], ch["cum_ref"], half, ch["rev"]))
                qt, kt = ch["q"] * ex, ch["kin"] * ex
            a = _dot_nt(qt.astype(BF16), kt.astype(BF16))
            ch["att"] = jnp.where(ch["level"] == li, a, ch["att"])
    return [ch["o"] + _dot(ch["att"].astype(BF16), ch["v_bf"]) for ch in chains]


def _hgrn_kernel(q_ref, zf_ref, zb_ref, v_ref, gate_ref, lbl_ref, gain_ref,
                 tril_ref, triu_ref, lvf_ref, lvb_ref, o_ref,
                 of_s, ob_s, stf_s, stb_s, cumf_s, cumb_s, *, layer, seq_len):
    c = HG_CHUNK
    lp = q_ref.shape[1]
    nc = lp // c
    depth = lbl_ref.shape[0]
    nheads = q_ref.shape[2] // LANES

    logits = [lbl_ref[d] for d in range(depth)]
    mx = functools.reduce(jnp.maximum, logits)
    es = [jnp.exp(x - mx) for x in logits]
    den = functools.reduce(lambda a, b: a + b, es)
    num = jnp.zeros_like(den)
    for d in range(1, layer + 1):
        num = num + es[d]
    lbs = num / den

    stf_s[...] = jnp.zeros_like(stf_s)
    stb_s[...] = jnp.zeros_like(stb_s)

    def body(n, carry):
        cf = lax.rem(n + nc - 1, nc)
        cb = lax.rem(2 * nc - 2 - n, nc)
        rf = pl.ds(pl.multiple_of(cf * c, c), c)
        rb = pl.ds(pl.multiple_of(cb * c, c), c)
        tri_f = tril_ref[(cf == nc - 1).astype(jnp.int32)]
        tri_b = triu_ref[(cb == nc - 1).astype(jnp.int32)]
        chains = []
        for hh in range(nheads):
            ln = pl.ds(hh * LANES, LANES)
            lanes = slice(hh * LANES, (hh + 1) * LANES)
            chains.append(dict(
                q=q_ref[0, rf, ln].astype(F32), z=zf_ref[0, rf, ln].astype(F32), v=v_ref[0, rf, ln].astype(F32),
                lb=lbs[0:1, lanes], st_ref=stf_s.at[hh], cum_ref=cumf_s.at[hh],
                tri=tri_f, level=lvf_ref[...], rev=False, out=(of_s, rf, ln)))
            chains.append(dict(
                q=q_ref[0, rb, ln].astype(F32), z=zb_ref[0, rb, ln].astype(F32), v=v_ref[0, rb, ln].astype(F32),
                lb=lbs[1:2, lanes], st_ref=stb_s.at[hh], cum_ref=cumb_s.at[hh],
                tri=tri_b, level=lvb_ref[...], rev=True, out=(ob_s, rb, ln)))
        for ch, o in zip(chains, _hgrn_chunks(chains)):
            dst, rows, ln = ch["out"]
            dst[rows, ln] = o
        return carry

    lax.fori_loop(0, nc, body, 0)

    def epilogue(n, carry):
        rows = pl.ds(pl.multiple_of(n * c, c), c)
        pos = n * c + lax.broadcasted_iota(jnp.int32, (c, LANES), 0)
        for hh in range(nheads):
            ln = pl.ds(hh * LANES, LANES)
            o = of_s[rows, ln] + ob_s[rows, ln]
            ms = jnp.mean(o * o, axis=-1, keepdims=True)
            y = o * lax.rsqrt(ms + NORM_EPS) * gain_ref[...] * _silu(gate_ref[0, rows, ln].astype(F32))
            o_ref[0, rows, ln] = jnp.where(pos < seq_len, y, 0.0).astype(o_ref.dtype)
        return carry

    lax.fori_loop(0, nc, epilogue, 0)


def _hgrn(p3d, lb_logits, gain, layer, seq_len):
    b, lp, _ = p3d.shape
    c = HG_CHUNK
    depth = lb_logits.shape[0]
    hps = HG_HEADS_PER_STEP
    wid = hps * LANES
    ngrp = HG_HEADS // hps
    tril, triu, lvf, lvb = _hgrn_constants(seq_len - (lp - c))
    col = lambda g: pl.BlockSpec((1, lp, wid), lambda i, h, g=g: (i, 0, g * ngrp + h))
    const = lambda shape: pl.BlockSpec(shape, lambda i, h: (0,) * len(shape))
    return pl.pallas_call(
        functools.partial(_hgrn_kernel, layer=layer, seq_len=seq_len),
        out_shape=jax.ShapeDtypeStruct((b, lp, HG_HEADS * HG_VDIM), BF16),
        grid=(b, ngrp),
        in_specs=[
            col(0), col(1), col(2), col(3), col(4),
            pl.BlockSpec((depth, 2, wid), lambda i, h: (0, 0, h)),
            const((1, HG_VDIM)),
            const((2, c, c)), const((2, c, c)), const((c, c)), const((c, c)),
        ],
        out_specs=pl.BlockSpec((1, lp, wid), lambda i, h: (i, 0, h)),
        scratch_shapes=[
            pltpu.VMEM((lp, wid), F32), pltpu.VMEM((lp, wid), F32),
            pltpu.VMEM((hps, HG_VDIM, HG_KDIM), F32), pltpu.VMEM((hps, HG_VDIM, HG_KDIM), F32),
            pltpu.VMEM((hps, c, HG_KDIM), F32), pltpu.VMEM((hps, c, HG_KDIM), F32),
        ],
        compiler_params=pltpu.CompilerParams(
            dimension_semantics=("parallel", "parallel"), vmem_limit_bytes=VMEM_LIMIT),
        name="hgrn2",
    )(p3d, p3d, p3d, p3d, p3d, lb_logits, gain.reshape(1, HG_VDIM), tril, triu, lvf, lvb)


def _rope_tables(positions):
    half = AT_HEAD_DIM // 2
    inv = ROPE_THETA ** (-jnp.arange(0, AT_HEAD_DIM, 2, dtype=F32) / AT_HEAD_DIM)
    ang = positions.astype(F32)[:, None] * inv[None, :]
    cos, sin = jnp.cos(ang), jnp.sin(ang)
    reps = LANES // half
    cos_t = jnp.tile(cos, (1, reps))
    sign = jnp.where((jnp.arange(LANES) % AT_HEAD_DIM) < half, -1.0, 1.0).astype(F32)
    sin_t = jnp.tile(sin, (1, reps)) * sign[None, :]
    return cos_t, sin_t


def _rope(t, cos_t, sin_t):
    half = AT_HEAD_DIM // 2
    lane = lax.broadcasted_iota(jnp.int32, t.shape, 1)
    first = (lane % AT_HEAD_DIM) < half
    partner = jnp.where(first, pltpu.roll(t, LANES - half, axis=1), pltpu.roll(t, half, axis=1))
    return t * cos_t + partner * sin_t


AT_TQ_CAP = 704


def _attn_tile(seq_len):
    best = None
    for t in range(16, min(seq_len, AT_TQ_CAP) + 1, 16):
        if seq_len % t == 0:
            best = t
    assert best is not None, seq_len
    return best


_BIAS_LANE = (AT_HEAD_DIM, 0)


def _attn_kernel(q_ref, k_ref, v_ref, gate_ref, cos_ref, sin_ref, lamv_ref, gain_ref, o_ref,
                 k_s, vaug_s, s_s, m_s, *, tq, lam_init, seq_len):
    lp = k_ref.shape[1]
    nq = seq_len // tq

    tk = _pick_tile(lp, AT_TQ_CAP, LANES)
    for n in range(lp // tk):
        rows = pl.ds(n * tk, tk)
        kr = _rope(k_ref[0, rows, :].astype(F32), cos_ref[rows, :], sin_ref[rows, :])
        lane = lax.broadcasted_iota(jnp.int32, (tk, LANES), 1)
        pos = n * tk + lax.broadcasted_iota(jnp.int32, (tk, LANES), 0)
        bias = jnp.where(pos < seq_len, 0.0, KEY_MASK)
        for c in range(2):
            in_half = (lane < AT_HEAD_DIM) if c == 0 else (lane >= AT_HEAD_DIM)
            kc = jnp.where(in_half, kr, jnp.where(lane == _BIAS_LANE[c], bias, 0.0))
            k_s[c, rows, :] = kc.astype(BF16)
        vaug_s[rows, 0:LANES] = v_ref[0, rows, :]
        vaug_s[rows, LANES:2 * LANES] = jnp.ones((tk, LANES), BF16)

    lv = lamv_ref[...]
    lam = (jnp.exp(jnp.sum(lv[0:1] * lv[1:2], axis=-1, keepdims=True))
           - jnp.exp(jnp.sum(lv[2:3] * lv[3:4], axis=-1, keepdims=True)) + lam_init)

    def scores(j, slot):
        rows = pl.ds(j * tq, tq)
        q = _rope(q_ref[0, rows, :].astype(F32), cos_ref[rows, :], sin_ref[rows, :])
        q = q * (AT_HEAD_DIM ** -0.5 * LOG2E)
        lane = lax.broadcasted_iota(jnp.int32, (tq, LANES), 1)
        for c in range(2):
            in_half = (lane < AT_HEAD_DIM) if c == 0 else (lane >= AT_HEAD_DIM)
            qc = jnp.where(in_half, q, jnp.where(lane == _BIAS_LANE[c], 1.0, 0.0)).astype(BF16)
            s = _dot_nt(qc, k_s[c])
            s_s[slot, c] = s
            m_s[slot, c] = jnp.max(s, axis=-1, keepdims=True)

    def outputs(j, slot):
        rows = pl.ds(j * tq, tq)
        outs = []
        for c in range(2):
            p = jnp.exp2(s_s[slot, c] - m_s[slot, c]).astype(BF16)
            ol = _dot(p, vaug_s[...])
            outs.append(ol[:, 0:LANES] / ol[:, LANES:2 * LANES])
        o = outs[0] - lam * outs[1]
        ms = jnp.mean(o * o, axis=-1, keepdims=True)
        y = (o * lax.rsqrt(ms + NORM_EPS) * gain_ref[...] * (1.0 - lam_init)
             * _silu(gate_ref[0, rows, :].astype(F32)))
        o_ref[0, rows, :] = y.astype(o_ref.dtype)

    scores(0, 0)
    for j in range(nq):
        if j + 1 < nq:
            scores(j + 1, (j + 1) % 2)
        outputs(j, j % 2)
    if seq_len < lp:
        o_ref[0, seq_len:lp, :] = jnp.zeros((lp - seq_len, LANES), o_ref.dtype)


def _attn(p3d, lam_vecs, gain, layer, seq_len, cos_t, sin_t):
    b, lp, _ = p3d.shape
    tq = _attn_tile(seq_len)
    lam_init = 0.8 - 0.6 * math.exp(-0.3 * layer)
    g0 = 5 * AT_HEADS
    col = lambda g: pl.BlockSpec((1, lp, LANES), lambda i, h, g=g: (i, 0, g0 + g * AT_HEADS + h))
    const = lambda shape: pl.BlockSpec(shape, lambda i, h: (0,) * len(shape))
    return pl.pallas_call(
        functools.partial(_attn_kernel, tq=tq, lam_init=lam_init, seq_len=seq_len),
        out_shape=jax.ShapeDtypeStruct((b, lp, AT_HEADS * 2 * AT_HEAD_DIM), BF16),
        grid=(b, AT_HEADS),
        in_specs=[
            col(0), col(1), col(2), col(3),
            const((lp, LANES)), const((lp, LANES)),
            const((4, AT_HEAD_DIM)), const((1, LANES)),
        ],
        out_specs=pl.BlockSpec((1, lp, LANES), lambda i, h: (i, 0, h)),
        scratch_shapes=[
            pltpu.VMEM((2, lp, LANES), BF16),
            pltpu.VMEM((lp, 2 * LANES), BF16),
            pltpu.VMEM((2, 2, tq, lp), F32),
            pltpu.VMEM((2, 2, tq, 1), F32),
        ],
        compiler_params=pltpu.CompilerParams(
            dimension_semantics=("parallel", "parallel"), vmem_limit_bytes=VMEM_LIMIT),
        name="diff_attn",
    )(p3d, p3d, p3d, p3d, cos_t, sin_t, lam_vecs, gain.reshape(1, LANES))


def _outproj_kernel(oh_ref, oa_ref, w_ref, h_ref, fg_ref, out_ref, *, final):
    wh = oh_ref.shape[1]
    acc = h_ref[...] + _dot(oh_ref[...], w_ref[0:wh, :]) + _dot(oa_ref[...], w_ref[wh:, :])
    if final:
        ms = jnp.mean(acc * acc, axis=-1, keepdims=True)
        acc = acc * lax.rsqrt(ms + NORM_EPS) * fg_ref[...]
    out_ref[...] = acc


def _outproj(o_h, o_a, w_bf16, layer, h3d, final_g, rows, final):
    b, _, d = h3d.shape
    wh, wa = o_h.shape[2], o_a.shape[2]
    tm = _pick_tile(rows, OUTPROJ_TM_CAP, 16)
    row_block = lambda w: pl.BlockSpec((None, tm, w), lambda i, j: (i, j, 0))
    return pl.pallas_call(
        functools.partial(_outproj_kernel, final=final),
        out_shape=jax.ShapeDtypeStruct((b, rows, d), F32),
        grid=(b, rows // tm),
        in_specs=[
            row_block(wh), row_block(wa),
            pl.BlockSpec((None, wh + wa, d), lambda i, j: (layer, 0, 0)),
            row_block(d),
            pl.BlockSpec((1, d), lambda i, j: (0, 0)),
        ],
        out_specs=row_block(d),
        compiler_params=pltpu.CompilerParams(
            dimension_semantics=("parallel", "parallel"), vmem_limit_bytes=VMEM_LIMIT),
        name="outproj",
    )(o_h, o_a, w_bf16, h3d, final_g.reshape(1, d))


def kernel(x, meta_tokens, norm_g, w_in, hg_lb_logits, hg_norm_g, diff_lambda, diff_subln_g,
           w_out, final_norm_g):
    bsz, seq, d = x.shape
    depth = w_in.shape[0]
    seq_len = N_META + seq
    lp = seq + HG_CHUNK
    assert seq % HG_CHUNK == 0 and N_META <= HG_CHUNK
    assert w_in.shape[2] == N_GROUPS * HG_HEADS * LANES

    tail = jnp.concatenate([meta_tokens.astype(x.dtype), jnp.zeros((lp - seq_len, d), x.dtype)])
    positions = jnp.concatenate([jnp.arange(N_META, seq_len), jnp.arange(N_META),
                                 jnp.zeros((lp - seq_len,), jnp.int32)])
    cos_t, sin_t = _rope_tables(positions)
    w_out_bf = w_out.astype(BF16)

    for l in range(depth):
        final = l == depth - 1
        if l == 0:
            p, h = _norm_inproj(x, tail, norm_g[l], w_in, l, emit_h=True)
        else:
            p, = _norm_inproj(h, h, norm_g[l], w_in, l, emit_h=False)
        o_h = _hgrn(p, hg_lb_logits, hg_norm_g[l], l, seq_len)
        o_a = _attn(p, diff_lambda[l], diff_subln_g[l], l, seq_len, cos_t, sin_t)
        h = _outproj(o_h, o_a, w_out_bf, l, h, final_norm_g, seq if final else lp, final)
    return h
```

```python
import functools
import math

import numpy as np
import jax
import jax.numpy as jnp
from jax import lax
from jax.experimental import pallas as pl
from jax.experimental.pallas import tpu as pltpu

F32 = jnp.float32
BF16 = jnp.bfloat16

N_META = 16
HG_HEADS = 8
HG_KDIM = 128
HG_VDIM = 128
AT_HEADS = 8
AT_HEAD_DIM = 64
ROPE_THETA = 10000.0
NORM_EPS = 1e-6
F_FLOOR = 1e-30
N_GROUPS = 9
LOG2E = math.log2(math.e)

LANES = 128
SUBLANES = 8
VMEM_LIMIT = 56 * 1024 * 1024

HG_CHUNK = 128
HG_LEVELS = (64, 32, 16, 8, 4, 2, 1)
HG_DIAG_LEVEL = len(HG_LEVELS)
HG_HEADS_PER_STEP = 4

KEY_MASK = -1e30
INPROJ_TN_CAP = 1024
OUTPROJ_TM_CAP = 576


def _pick_tile(n, cap, mult):
    best = None
    for t in range(mult, min(n, cap) + 1, mult):
        if n % t == 0:
            best = t
    assert best is not None, (n, cap, mult)
    return best


def _dot(a, b):
    return jnp.dot(a, b, preferred_element_type=F32)


def _dot_nt(a, b):
    return lax.dot_general(a, b, (((1,), (1,)), ((), ())), preferred_element_type=F32)


def _silu(g):
    return g * (1.0 / (1.0 + jnp.exp(-g)))


def _norm_inproj_kernel(body_ref, tail_ref, g_ref, w_ref, *rest, emit_h):
    if emit_h:
        body_col_ref, tail_col_ref, p_ref, h_ref, u_ref = rest
    else:
        p_ref, u_ref = rest
    nbody = body_ref.shape[0]

    @pl.when(pl.program_id(1) == 0)
    def _():
        for src, rows in ((body_ref, slice(0, nbody)), (tail_ref, slice(nbody, None))):
            x = src[...]
            ms = jnp.mean(x * x, axis=-1, keepdims=True)
            u_ref[rows, :] = (x * lax.rsqrt(ms + NORM_EPS) * g_ref[...]).astype(BF16)

    if emit_h:
        h_ref[0:nbody, :] = body_col_ref[...]
        h_ref[nbody:, :] = tail_col_ref[...]
    p_ref[...] = _dot(u_ref[...], w_ref[...].astype(BF16)).astype(BF16)


def _norm_inproj(body, tail, g, w, layer, emit_h):
    b, d = body.shape[0], body.shape[2]
    ntail = HG_CHUNK
    nbody = body.shape[1] if tail.ndim == 2 else body.shape[1] - ntail
    lp = nbody + ntail
    n = w.shape[2]
    tn = _pick_tile(n, INPROJ_TN_CAP, LANES)
    ncol = d // LANES
    if tail.ndim == 2:
        tail_spec = pl.BlockSpec((ntail, d), lambda i, j: (0, 0))
    else:
        tail_spec = pl.BlockSpec((None, ntail, d), lambda i, j: (i, nbody // ntail, 0))
    in_specs = [
        pl.BlockSpec((None, nbody, d), lambda i, j: (i, 0, 0)),
        tail_spec,
        pl.BlockSpec((1, d), lambda i, j: (0, 0)),
        pl.BlockSpec((None, d, tn), lambda i, j: (layer, 0, j)),
    ]
    operands = [body, tail, g.reshape(1, d), w]
    out_shape = [jax.ShapeDtypeStruct((b, lp, n), BF16)]
    out_specs = [pl.BlockSpec((None, lp, tn), lambda i, j: (i, 0, j))]
    if emit_h:
        assert tail.ndim == 2 and n // tn >= ncol
        col = lambda j: jnp.minimum(j, ncol - 1)
        in_specs += [pl.BlockSpec((None, nbody, LANES), lambda i, j: (i, 0, col(j))),
                     pl.BlockSpec((ntail, LANES), lambda i, j: (0, col(j)))]
        operands += [body, tail]
        out_shape.append(jax.ShapeDtypeStruct((b, lp, d), F32))
        out_specs.append(pl.BlockSpec((None, lp, LANES), lambda i, j: (i, 0, col(j))))
    return pl.pallas_call(
        functools.partial(_norm_inproj_kernel, emit_h=emit_h),
        out_shape=out_shape,
        grid=(b, n // tn),
        in_specs=in_specs,
        out_specs=out_specs,
        scratch_shapes=[pltpu.VMEM((lp, d), BF16)],
        compiler_params=pltpu.CompilerParams(
            dimension_semantics=("parallel", "arbitrary"), vmem_limit_bytes=VMEM_LIMIT),
        name="norm_inproj",
    )(*operands)


def _hgrn_constants(last_valid):
    c = HG_CHUNK
    t = np.arange(c)[:, None]
    s = np.arange(c)[None, :]
    tril = (s <= t).astype(np.float32)
    real = (np.arange(c) < last_valid).astype(np.float32)[None, :]
    tril = np.stack([tril, tril * real])
    triu = np.stack([tril[0].T, tril[0].T * real])
    x = t ^ s
    top = np.where(x > 0, np.floor(np.log2(np.maximum(x, 1))).astype(np.int64), 0)
    half = 1 << top
    level = np.full((c, c), -1, np.int32)
    for li, hh in enumerate(HG_LEVELS):
        level = np.where((t > s) & (half == hh), li, level)
    level = np.where(t == s, HG_DIAG_LEVEL, level).astype(np.int32)
    return (jnp.asarray(tril, BF16), jnp.asarray(triu, BF16),
            jnp.asarray(level), jnp.asarray(level.T))


def _split2(x):
    hi = x.astype(BF16)
    return hi, (x - hi.astype(F32)).astype(BF16)


def _level_operands(q, kin, cum, cum_ref, half, rev):
    c = HG_CHUNK
    blk = 2 * half
    off = half if rev else half - 1
    zeros = jnp.zeros((half, LANES), F32)
    qt, kt = [], []
    for m in range(c // blk):
        ref = cum_ref[pl.ds(m * blk + off, 1), :]
        lo = slice(m * blk, m * blk + half)
        hi = slice(m * blk + half, (m + 1) * blk)
        if rev:
            qt += [q[lo] * jnp.exp2(cum[lo] - ref), zeros]
            kt += [zeros, kin[hi] * jnp.exp2(ref - cum[hi])]
        else:
            kt += [kin[lo] * jnp.exp2(ref - cum[lo]), zeros]
            qt += [zeros, q[hi] * jnp.exp2(cum[hi] - ref)]
    return jnp.concatenate(qt, axis=0), jnp.concatenate(kt, axis=0)


def _level_exponent(cum, cum_ref, half, rev):
    c = HG_CHUNK
    blk = 2 * half
    off = half if rev else half - 1
    parts = []
    assert half < SUBLANES
    sub = lax.broadcasted_iota(jnp.int32, (SUBLANES, LANES), 0)
    for g in range(c // SUBLANES):
        base = g * SUBLANES
        ref = jnp.broadcast_to(cum_ref[pl.ds(base + off, 1), :], (SUBLANES, LANES))
        if blk < SUBLANES:
            hi = jnp.broadcast_to(cum_ref[pl.ds(base + blk + off, 1), :], (SUBLANES, LANES))
            ref = jnp.where(sub < blk, ref, hi)
        parts.append(-jnp.abs(cum[base:base + SUBLANES] - ref))
    return jnp.concatenate(parts, axis=0)


def _hgrn_chunks(chains):
    c = HG_CHUNK
    row = lax.broadcasted_iota(jnp.int32, (c, LANES), 0)
    for ch in chains:
        sig = 1.0 / (1.0 + jnp.exp(-ch["z"]))
        f = ch["lb"] + (1.0 - ch["lb"]) * sig
        ch["kin"] = 1.0 - f
        ch["f"] = jnp.maximum(f, F_FLOOR)
        ch["parts"] = _split2(jnp.log(ch["f"]))
    for ch in chains:
        hi, lo = ch["parts"]
        cum = (_dot(ch["tri"], hi) + _dot(ch["tri"], lo)) * LOG2E
        ch["cum_ref"][...] = cum
        ch["cum"] = cum
        ch["tot"] = cum[0:1] if ch["rev"] else cum[c - 1:c]
    for ch in chains:
        cum, tot = ch["cum"], ch["tot"]
        q_dec = ch["q"] * jnp.exp2(cum)
        k_dec = ch["kin"] * jnp.exp2(tot - cum)
        st = ch["st_ref"][...]
        ch["o"] = _dot_nt(q_dec.astype(BF16), st.astype(BF16))
        ch["v_bf"] = ch["v"].astype(BF16)
        ch["st_ref"][...] = st * jnp.exp2(tot) + _dot(ch["v"].T.astype(BF16), k_dec.astype(BF16))
        ch["att"] = jnp.where(ch["level"] == HG_DIAG_LEVEL,
                              _dot_nt(ch["q"].astype(BF16), ch["kin"].astype(BF16)), 0.0)
    for li, half in enumerate(HG_LEVELS):
        for ch in chains:
            if half >= SUBLANES:
                qt, kt = _level_operands(ch["q"], ch["kin"], ch["cum"], ch["cum_ref"], half, ch["rev"])
            else:
                if half == 1:
                    q_side = (row % 2 == 0) if ch["rev"] else (row % 2 == 1)
                    ex = jnp.where(q_side, ch["f"], 1.0)
                else:
                    ex = jnp.exp2(_level_exponent(ch["cum"], ch["cum_ref"], half, ch["rev"]))
                qt, kt = ch["q"] * ex, ch["kin"] * ex
            a = _dot_nt(qt.astype(BF16), kt.astype(BF16))
            ch["att"] = jnp.where(ch["level"] == li, a, ch["att"])
    return [ch["o"] + _dot(ch["att"].astype(BF16), ch["v_bf"]) for ch in chains]


def _hgrn_kernel(q_ref, zf_ref, zb_ref, v_ref, gate_ref, lbl_ref, gain_ref,
                 tril_ref, triu_ref, lvf_ref, lvb_ref, o_ref,
                 of_s, ob_s, stf_s, stb_s, cumf_s, cumb_s, *, layer, seq_len):
    c = HG_CHUNK
    lp = q_ref.shape[1]
    nc = lp // c
    depth = lbl_ref.shape[0]
    nheads = q_ref.shape[2] // LANES

    logits = [lbl_ref[d] for d in range(depth)]
    mx = functools.reduce(jnp.maximum, logits)
    es = [jnp.exp(x - mx) for x in logits]
    den = functools.reduce(lambda a, b: a + b, es)
    num = jnp.zeros_like(den)
    for d in range(1, layer + 1):
        num = num + es[d]
    lbs = num / den

    stf_s[...] = jnp.zeros_like(stf_s)
    stb_s[...] = jnp.zeros_like(stb_s)

    def body(n, carry):
        cf = lax.rem(n + nc - 1, nc)
        cb = lax.rem(2 * nc - 2 - n, nc)
        rf = pl.ds(pl.multiple_of(cf * c, c), c)
        rb = pl.ds(pl.multiple_of(cb * c, c), c)
        tri_f = tril_ref[(cf == nc - 1).astype(jnp.int32)]
        tri_b = triu_ref[(cb == nc - 1).astype(jnp.int32)]
        chains = []
        for hh in range(nheads):
            ln = pl.ds(hh * LANES, LANES)
            lanes = slice(hh * LANES, (hh + 1) * LANES)
            chains.append(dict(
                q=q_ref[0, rf, ln].astype(F32), z=zf_ref[0, rf, ln].astype(F32), v=v_ref[0, rf, ln].astype(F32),
                lb=lbs[0:1, lanes], st_ref=stf_s.at[hh], cum_ref=cumf_s.at[hh],
                tri=tri_f, level=lvf_ref[...], rev=False, out=(of_s, rf, ln)))
            chains.append(dict(
                q=q_ref[0, rb, ln].astype(F32), z=zb_ref[0, rb, ln].astype(F32), v=v_ref[0, rb, ln].astype(F32),
                lb=lbs[1:2, lanes], st_ref=stb_s.at[hh], cum_ref=cumb_s.at[hh],
                tri=tri_b, level=lvb_ref[...], rev=True, out=(ob_s, rb, ln)))
        for ch, o in zip(chains, _hgrn_chunks(chains)):
            dst, rows, ln = ch["out"]
            dst[rows, ln] = o
        return carry

    lax.fori_loop(0, nc, body, 0)

    def epilogue(n, carry):
        rows = pl.ds(pl.multiple_of(n * c, c), c)
        pos = n * c + lax.broadcasted_iota(jnp.int32, (c, LANES), 0)
        for hh in range(nheads):
            ln = pl.ds(hh * LANES, LANES)
            o = of_s[rows, ln] + ob_s[rows, ln]
            ms = jnp.mean(o * o, axis=-1, keepdims=True)
            y = o * lax.rsqrt(ms + NORM_EPS) * gain_ref[...] * _silu(gate_ref[0, rows, ln].astype(F32))
            o_ref[0, rows, ln] = jnp.where(pos < seq_len, y, 0.0).astype(o_ref.dtype)
        return carry

    lax.fori_loop(0, nc, epilogue, 0)


def _hgrn(p3d, lb_logits, gain, layer, seq_len):
    b, lp, _ = p3d.shape
    c = HG_CHUNK
    depth = lb_logits.shape[0]
    hps = HG_HEADS_PER_STEP
    wid = hps * LANES
    ngrp = HG_HEADS // hps
    tril, triu, lvf, lvb = _hgrn_constants(seq_len - (lp - c))
    col = lambda g: pl.BlockSpec((1, lp, wid), lambda i, h, g=g: (i, 0, g * ngrp + h))
    const = lambda shape: pl.BlockSpec(shape, lambda i, h: (0,) * len(shape))
    return pl.pallas_call(
        functools.partial(_hgrn_kernel, layer=layer, seq_len=seq_len),
        out_shape=jax.ShapeDtypeStruct((b, lp, HG_HEADS * HG_VDIM), BF16),
        grid=(b, ngrp),
        in_specs=[
            col(0), col(1), col(2), col(3), col(4),
            pl.BlockSpec((depth, 2, wid), lambda i, h: (0, 0, h)),
            const((1, HG_VDIM)),
            const((2, c, c)), const((2, c, c)), const((c, c)), const((c, c)),
        ],
        out_specs=pl.BlockSpec((1, lp, wid), lambda i, h: (i, 0, h)),
        scratch_shapes=[
            pltpu.VMEM((lp, wid), F32), pltpu.VMEM((lp, wid), F32),
            pltpu.VMEM((hps, HG_VDIM, HG_KDIM), F32), pltpu.VMEM((hps, HG_VDIM, HG_KDIM), F32),
            pltpu.VMEM((hps, c, HG_KDIM), F32), pltpu.VMEM((hps, c, HG_KDIM), F32),
        ],
        compiler_params=pltpu.CompilerParams(
            dimension_semantics=("parallel", "parallel"), vmem_limit_bytes=VMEM_LIMIT),
        name="hgrn2",
    )(p3d, p3d, p3d, p3d, p3d, lb_logits, gain.reshape(1, HG_VDIM), tril, triu, lvf, lvb)


def _rope_tables(positions):
    half = AT_HEAD_DIM // 2
    inv = ROPE_THETA ** (-jnp.arange(0, AT_HEAD_DIM, 2, dtype=F32) / AT_HEAD_DIM)
    ang = positions.astype(F32)[:, None] * inv[None, :]
    cos, sin = jnp.cos(ang), jnp.sin(ang)
    reps = LANES // half
    cos_t = jnp.tile(cos, (1, reps))
    sign = jnp.where((jnp.arange(LANES) % AT_HEAD_DIM) < half, -1.0, 1.0).astype(F32)
    sin_t = jnp.tile(sin, (1, reps)) * sign[None, :]
    return cos_t, sin_t


def _rope(t, cos_t, sin_t):
    half = AT_HEAD_DIM // 2
    lane = lax.broadcasted_iota(jnp.int32, t.shape, 1)
    first = (lane % AT_HEAD_DIM) < half
    partner = jnp.where(first, pltpu.roll(t, LANES - half, axis=1), pltpu.roll(t, half, axis=1))
    return t * cos_t + partner * sin_t


AT_TQ_CAP = 704


def _attn_tile(seq_len):
    best = None
    for t in range(16, min(seq_len, AT_TQ_CAP) + 1, 16):
        if seq_len % t == 0:
            best = t
    assert best is not None, seq_len
    return best


def _attn_kernel(q_ref, k_ref, v_ref, gate_ref, cos_ref, sin_ref, lamv_ref, gain_ref, o_ref,
                 k_s, kt_s, vaug_s, vt_s, s_s, m_s, *, tq, lam_init, seq_len):
    lp = k_ref.shape[1]
    nq = seq_len // tq
    nmain = lp - LANES
    ntail = seq_len - nmain
    assert 0 < ntail <= LANES

    tk = _pick_tile(nmain, AT_TQ_CAP, LANES)
    for n in range(nmain // tk):
        rows = pl.ds(n * tk, tk)
        kr = _rope(k_ref[0, rows, :].astype(F32), cos_ref[rows, :], sin_ref[rows, :])
        lane = lax.broadcasted_iota(jnp.int32, (tk, LANES), 1)
        for c in range(2):
            in_half = (lane < AT_HEAD_DIM) if c == 0 else (lane >= AT_HEAD_DIM)
            k_s[c, rows, :] = jnp.where(in_half, kr, 0.0).astype(BF16)
        vaug_s[rows, 0:LANES] = v_ref[0, rows, :]
        vaug_s[rows, LANES:2 * LANES] = jnp.ones((tk, LANES), BF16)
    rows = pl.ds(nmain, LANES)
    kr = _rope(k_ref[0, rows, :].astype(F32), cos_ref[rows, :], sin_ref[rows, :])
    lane = lax.broadcasted_iota(jnp.int32, (LANES, LANES), 1)
    kt_s[0:LANES, :] = jnp.where(lane < AT_HEAD_DIM, kr, 0.0).astype(BF16)
    kt_s[LANES:2 * LANES, :] = jnp.where(lane >= AT_HEAD_DIM, kr, 0.0).astype(BF16)
    vt = v_ref[0, rows, :]
    zeros = jnp.zeros((LANES, LANES), BF16)
    vt_s[0:LANES, 0:LANES] = vt
    vt_s[0:LANES, LANES:2 * LANES] = zeros
    vt_s[LANES:2 * LANES, 0:LANES] = zeros
    vt_s[LANES:2 * LANES, LANES:2 * LANES] = vt

    lv = lamv_ref[...]
    lam = (jnp.exp(jnp.sum(lv[0:1] * lv[1:2], axis=-1, keepdims=True))
           - jnp.exp(jnp.sum(lv[2:3] * lv[3:4], axis=-1, keepdims=True)) + lam_init)

    def scores(j, slot):
        rows = pl.ds(j * tq, tq)
        q = _rope(q_ref[0, rows, :].astype(F32), cos_ref[rows, :], sin_ref[rows, :])
        q = q * (AT_HEAD_DIM ** -0.5 * LOG2E)
        lane = lax.broadcasted_iota(jnp.int32, (tq, LANES), 1)
        tail = _dot_nt(q.astype(BF16), kt_s[...])
        for c in range(2):
            in_half = (lane < AT_HEAD_DIM) if c == 0 else (lane >= AT_HEAD_DIM)
            s = _dot_nt(jnp.where(in_half, q, 0.0).astype(BF16), k_s[c])
            st = jnp.where(lane < ntail, tail[:, c * LANES:(c + 1) * LANES], KEY_MASK)
            s_s[slot, c, :, 0:nmain] = s
            s_s[slot, c, :, nmain:lp] = st
            m_s[slot, c] = jnp.maximum(jnp.max(s, axis=-1, keepdims=True),
                                       jnp.max(st, axis=-1, keepdims=True))

    def outputs(j, slot):
        rows = pl.ds(j * tq, tq)
        mains, tails = [], []
        for c in range(2):
            m = m_s[slot, c]
            p = jnp.exp2(s_s[slot, c, :, 0:nmain] - m).astype(BF16)
            mains.append(_dot(p, vaug_s[...]))
            tails.append(jnp.exp2(s_s[slot, c, :, nmain:lp] - m).astype(BF16))
        ot = _dot(jnp.concatenate(tails, axis=1), vt_s[...])
        outs = []
        for c in range(2):
            lt = jnp.sum(tails[c].astype(F32), axis=-1, keepdims=True)
            outs.append((mains[c][:, 0:LANES] + ot[:, c * LANES:(c + 1) * LANES])
                        / (mains[c][:, LANES:2 * LANES] + lt))
        o = outs[0] - lam * outs[1]
        ms = jnp.mean(o * o, axis=-1, keepdims=True)
        y = (o * lax.rsqrt(ms + NORM_EPS) * gain_ref[...] * (1.0 - lam_init)
             * _silu(gate_ref[0, rows, :].astype(F32)))
        o_ref[0, rows, :] = y.astype(o_ref.dtype)

    scores(0, 0)
    for j in range(nq):
        if j + 1 < nq:
            scores(j + 1, (j + 1) % 2)
        outputs(j, j % 2)
    if seq_len < lp:
        o_ref[0, seq_len:lp, :] = jnp.zeros((lp - seq_len, LANES), o_ref.dtype)


def _attn(p3d, lam_vecs, gain, layer, seq_len, cos_t, sin_t):
    b, lp, _ = p3d.shape
    tq = _attn_tile(seq_len)
    lam_init = 0.8 - 0.6 * math.exp(-0.3 * layer)
    g0 = 5 * AT_HEADS
    col = lambda g: pl.BlockSpec((1, lp, LANES), lambda i, h, g=g: (i, 0, g0 + g * AT_HEADS + h))
    const = lambda shape: pl.BlockSpec(shape, lambda i, h: (0,) * len(shape))
    return pl.pallas_call(
        functools.partial(_attn_kernel, tq=tq, lam_init=lam_init, seq_len=seq_len),
        out_shape=jax.ShapeDtypeStruct((b, lp, AT_HEADS * 2 * AT_HEAD_DIM), BF16),
        grid=(b, AT_HEADS),
        in_specs=[
            col(0), col(1), col(2), col(3),
            const((lp, LANES)), const((lp, LANES)),
            const((4, AT_HEAD_DIM)), const((1, LANES)),
        ],
        out_specs=pl.BlockSpec((1, lp, LANES), lambda i, h: (i, 0, h)),
        scratch_shapes=[
            pltpu.VMEM((2, lp - LANES, LANES), BF16),
            pltpu.VMEM((2 * LANES, LANES), BF16),
            pltpu.VMEM((lp - LANES, 2 * LANES), BF16),
            pltpu.VMEM((2 * LANES, 2 * LANES), BF16),
            pltpu.VMEM((2, 2, tq, lp), F32),
            pltpu.VMEM((2, 2, tq, 1), F32),
        ],
        compiler_params=pltpu.CompilerParams(
            dimension_semantics=("parallel", "parallel"), vmem_limit_bytes=VMEM_LIMIT),
        name="diff_attn",
    )(p3d, p3d, p3d, p3d, cos_t, sin_t, lam_vecs, gain.reshape(1, LANES))


def _outproj_kernel(oh_ref, oa_ref, w_ref, h_ref, fg_ref, out_ref, *, final):
    wh = oh_ref.shape[1]
    acc = h_ref[...] + _dot(oh_ref[...], w_ref[0:wh, :]) + _dot(oa_ref[...], w_ref[wh:, :])
    if final:
        ms = jnp.mean(acc * acc, axis=-1, keepdims=True)
        acc = acc * lax.rsqrt(ms + NORM_EPS) * fg_ref[...]
    out_ref[...] = acc


def _outproj(o_h, o_a, w_bf16, layer, h3d, final_g, rows, final):
    b, _, d = h3d.shape
    wh, wa = o_h.shape[2], o_a.shape[2]
    tm = _pick_tile(rows, OUTPROJ_TM_CAP, 16)
    row_block = lambda w: pl.BlockSpec((None, tm, w), lambda i, j: (i, j, 0))
    return pl.pallas_call(
        functools.partial(_outproj_kernel, final=final),
        out_shape=jax.ShapeDtypeStruct((b, rows, d), F32),
        grid=(b, rows // tm),
        in_specs=[
            row_block(wh), row_block(wa),
            pl.BlockSpec((None, wh + wa, d), lambda i, j: (layer, 0, 0)),
            row_block(d),
            pl.BlockSpec((1, d), lambda i, j: (0, 0)),
        ],
        out_specs=row_block(d),
        compiler_params=pltpu.CompilerParams(
            dimension_semantics=("parallel", "parallel"), vmem_limit_bytes=VMEM_LIMIT),
        name="outproj",
    )(o_h, o_a, w_bf16, h3d, final_g.reshape(1, d))


def kernel(x, meta_tokens, norm_g, w_in, hg_lb_logits, hg_norm_g, diff_lambda, diff_subln_g,
           w_out, final_norm_g):
    bsz, seq, d = x.shape
    depth = w_in.shape[0]
    seq_len = N_META + seq
    lp = seq + HG_CHUNK
    assert seq % HG_CHUNK == 0 and N_META <= HG_CHUNK
    assert w_in.shape[2] == N_GROUPS * HG_HEADS * LANES

    tail = jnp.concatenate([meta_tokens.astype(x.dtype), jnp.zeros((lp - seq_len, d), x.dtype)])
    positions = jnp.concatenate([jnp.arange(N_META, seq_len), jnp.arange(N_META),
                                 jnp.zeros((lp - seq_len,), jnp.int32)])
    cos_t, sin_t = _rope_tables(positions)
    w_out_bf = w_out.astype(BF16)

    for l in range(depth):
        final = l == depth - 1
        if l == 0:
            p, h = _norm_inproj(x, tail, norm_g[l], w_in, l, emit_h=True)
        else:
            p, = _norm_inproj(h, h, norm_g[l], w_in, l, emit_h=False)
        o_h = _hgrn(p, hg_lb_logits, hg_norm_g[l], l, seq_len)
        o_a = _attn(p, diff_lambda[l], diff_subln_g[l], l, seq_len, cos_t, sin_t)
        h = _outproj(o_h, o_a, w_out_bf, l, h, final_norm_g, seq if final else lp, final)
    return h
```

```python
import functools
import math

import numpy as np
import jax
import jax.numpy as jnp
from jax import lax
from jax.experimental import pallas as pl
from jax.experimental.pallas import tpu as pltpu

F32 = jnp.float32
BF16 = jnp.bfloat16

N_META = 16
HG_HEADS = 8
HG_KDIM = 128
HG_VDIM = 128
AT_HEADS = 8
AT_HEAD_DIM = 64
ROPE_THETA = 10000.0
NORM_EPS = 1e-6
F_FLOOR = 1e-30
N_GROUPS = 9
LOG2E = math.log2(math.e)

LANES = 128
SUBLANES = 8
VMEM_LIMIT = 56 * 1024 * 1024

HG_CHUNK = 128
HG_LEVELS = (64, 32, 16, 8, 4, 2, 1)
HG_DIAG_LEVEL = len(HG_LEVELS)
HG_HEADS_PER_STEP = 4

KEY_MASK = -1e30
INPROJ_TN_CAP = 1024
OUTPROJ_TM_CAP = 576


def _pick_tile(n, cap, mult):
    best = None
    for t in range(mult, min(n, cap) + 1, mult):
        if n % t == 0:
            best = t
    assert best is not None, (n, cap, mult)
    return best


def _dot(a, b):
    return jnp.dot(a, b, preferred_element_type=F32)


def _dot_nt(a, b):
    return lax.dot_general(a, b, (((1,), (1,)), ((), ())), preferred_element_type=F32)


def _silu(g):
    return g * (1.0 / (1.0 + jnp.exp(-g)))


def _norm_inproj_kernel(body_ref, tail_ref, g_ref, w_ref, *rest, emit_h, nreal):
    if emit_h:
        body_col_ref, tail_col_ref, p_ref, h_ref, u_ref = rest
    else:
        p_ref, u_ref = rest
    nbody = body_ref.shape[0]

    @pl.when(pl.program_id(1) == 0)
    def _():
        for src, rows in ((body_ref, slice(0, nbody)), (tail_ref, slice(nbody, None))):
            x = src[...]
            ms = jnp.mean(x * x, axis=-1, keepdims=True)
            u_ref[rows, :] = (x * lax.rsqrt(ms + NORM_EPS) * g_ref[...]).astype(BF16)

    if emit_h:
        h_ref[0:nbody, :] = body_col_ref[...]
        h_ref[nbody:, :] = tail_col_ref[...]
    p_ref[0:nreal, :] = _dot(u_ref[0:nreal, :], w_ref[...].astype(BF16)).astype(BF16)
    p_ref[nreal:, :] = jnp.zeros((p_ref.shape[0] - nreal, p_ref.shape[1]), BF16)


def _norm_inproj(body, tail, g, w, layer, nreal, emit_h):
    b, d = body.shape[0], body.shape[2]
    ntail = HG_CHUNK
    nbody = body.shape[1] if tail.ndim == 2 else body.shape[1] - ntail
    lp = nbody + ntail
    n = w.shape[2]
    tn = _pick_tile(n, INPROJ_TN_CAP, LANES)
    ncol = d // LANES
    if tail.ndim == 2:
        tail_spec = pl.BlockSpec((ntail, d), lambda i, j: (0, 0))
    else:
        tail_spec = pl.BlockSpec((None, ntail, d), lambda i, j: (i, nbody // ntail, 0))
    in_specs = [
        pl.BlockSpec((None, nbody, d), lambda i, j: (i, 0, 0)),
        tail_spec,
        pl.BlockSpec((1, d), lambda i, j: (0, 0)),
        pl.BlockSpec((None, d, tn), lambda i, j: (layer, 0, j)),
    ]
    operands = [body, tail, g.reshape(1, d), w]
    out_shape = [jax.ShapeDtypeStruct((b, lp, n), BF16)]
    out_specs = [pl.BlockSpec((None, lp, tn), lambda i, j: (i, 0, j))]
    if emit_h:
        assert tail.ndim == 2 and n // tn >= ncol
        col = lambda j: jnp.minimum(j, ncol - 1)
        in_specs += [pl.BlockSpec((None, nbody, LANES), lambda i, j: (i, 0, col(j))),
                     pl.BlockSpec((ntail, LANES), lambda i, j: (0, col(j)))]
        operands += [body, tail]
        out_shape.append(jax.ShapeDtypeStruct((b, lp, d), F32))
        out_specs.append(pl.BlockSpec((None, lp, LANES), lambda i, j: (i, 0, col(j))))
    return pl.pallas_call(
        functools.partial(_norm_inproj_kernel, emit_h=emit_h, nreal=nreal),
        out_shape=out_shape,
        grid=(b, n // tn),
        in_specs=in_specs,
        out_specs=out_specs,
        scratch_shapes=[pltpu.VMEM((lp, d), BF16)],
        compiler_params=pltpu.CompilerParams(
            dimension_semantics=("parallel", "arbitrary"), vmem_limit_bytes=VMEM_LIMIT),
        name="norm_inproj",
    )(*operands)


def _hgrn_constants(last_valid):
    c = HG_CHUNK
    t = np.arange(c)[:, None]
    s = np.arange(c)[None, :]
    tril = (s <= t).astype(np.float32)
    real = (np.arange(c) < last_valid).astype(np.float32)[None, :]
    tril = np.stack([tril, tril * real])
    triu = np.stack([tril[0].T, tril[0].T * real])
    x = t ^ s
    top = np.where(x > 0, np.floor(np.log2(np.maximum(x, 1))).astype(np.int64), 0)
    half = 1 << top
    level = np.full((c, c), -1, np.int32)
    for li, hh in enumerate(HG_LEVELS):
        level = np.where((t > s) & (half == hh), li, level)
    level = np.where(t == s, HG_DIAG_LEVEL, level).astype(np.int32)
    return (jnp.asarray(tril, BF16), jnp.asarray(triu, BF16),
            jnp.asarray(level), jnp.asarray(level.T))


def _split2(x):
    hi = x.astype(BF16)
    return hi, (x - hi.astype(F32)).astype(BF16)


def _level_operands(q, kin, cum, cum_ref, half, rev):
    c = HG_CHUNK
    blk = 2 * half
    off = half if rev else half - 1
    zeros = jnp.zeros((half, LANES), F32)
    qt, kt = [], []
    for m in range(c // blk):
        ref = cum_ref[pl.ds(m * blk + off, 1), :]
        lo = slice(m * blk, m * blk + half)
        hi = slice(m * blk + half, (m + 1) * blk)
        if rev:
            qt += [q[lo] * jnp.exp2(cum[lo] - ref), zeros]
            kt += [zeros, kin[hi] * jnp.exp2(ref - cum[hi])]
        else:
            kt += [kin[lo] * jnp.exp2(ref - cum[lo]), zeros]
            qt += [zeros, q[hi] * jnp.exp2(cum[hi] - ref)]
    return jnp.concatenate(qt, axis=0), jnp.concatenate(kt, axis=0)


def _level_exponent(cum, cum_ref, half, rev):
    c = HG_CHUNK
    blk = 2 * half
    off = half if rev else half - 1
    parts = []
    assert half < SUBLANES
    sub = lax.broadcasted_iota(jnp.int32, (SUBLANES, LANES), 0)
    for g in range(c // SUBLANES):
        base = g * SUBLANES
        ref = jnp.broadcast_to(cum_ref[pl.ds(base + off, 1), :], (SUBLANES, LANES))
        if blk < SUBLANES:
            hi = jnp.broadcast_to(cum_ref[pl.ds(base + blk + off, 1), :], (SUBLANES, LANES))
            ref = jnp.where(sub < blk, ref, hi)
        parts.append(-jnp.abs(cum[base:base + SUBLANES] - ref))
    return jnp.concatenate(parts, axis=0)


def _hgrn_chunks(chains):
    c = HG_CHUNK
    row = lax.broadcasted_iota(jnp.int32, (c, LANES), 0)
    for ch in chains:
        sig = 1.0 / (1.0 + jnp.exp(-ch["z"]))
        f = ch["lb"] + (1.0 - ch["lb"]) * sig
        ch["kin"] = 1.0 - f
        ch["f"] = jnp.maximum(f, F_FLOOR)
        ch["parts"] = _split2(jnp.log(ch["f"]))
    for ch in chains:
        hi, lo = ch["parts"]
        cum = (_dot(ch["tri"], hi) + _dot(ch["tri"], lo)) * LOG2E
        ch["cum_ref"][...] = cum
        ch["cum"] = cum
        ch["tot"] = cum[0:1] if ch["rev"] else cum[c - 1:c]
    for ch in chains:
        cum, tot = ch["cum"], ch["tot"]
        q_dec = ch["q"] * jnp.exp2(cum)
        k_dec = ch["kin"] * jnp.exp2(tot - cum)
        st = ch["st_ref"][...]
        ch["o"] = _dot_nt(q_dec.astype(BF16), st.astype(BF16))
        ch["v_bf"] = ch["v"].astype(BF16)
        ch["st_ref"][...] = st * jnp.exp2(tot) + _dot(ch["v"].T.astype(BF16), k_dec.astype(BF16))
        ch["att"] = jnp.where(ch["level"] == HG_DIAG_LEVEL,
                              _dot_nt(ch["q"].astype(BF16), ch["kin"].astype(BF16)), 0.0)
    for li, half in enumerate(HG_LEVELS):
        for ch in chains:
            if half >= SUBLANES:
                qt, kt = _level_operands(ch["q"], ch["kin"], ch["cum"], ch["cum_ref"], half, ch["rev"])
            else:
                if half == 1:
                    q_side = (row % 2 == 0) if ch["rev"] else (row % 2 == 1)
                    ex = jnp.where(q_side, ch["f"], 1.0)
                else:
                    ex = jnp.exp2(_level_exponent(ch["cum"], ch["cum_ref"], half, ch["rev"]))
                qt, kt = ch["q"] * ex, ch["kin"] * ex
            a = _dot_nt(qt.astype(BF16), kt.astype(BF16))
            ch["att"] = jnp.where(ch["level"] == li, a, ch["att"])
    return [ch["o"] + _dot(ch["att"].astype(BF16), ch["v_bf"]) for ch in chains]


def _hgrn_kernel(q_ref, zf_ref, zb_ref, v_ref, gate_ref, lbl_ref, gain_ref,
                 tril_ref, triu_ref, lvf_ref, lvb_ref, o_ref,
                 of_s, ob_s, stf_s, stb_s, cumf_s, cumb_s, *, layer, seq_len):
    c = HG_CHUNK
    lp = q_ref.shape[1]
    nc = lp // c
    depth = lbl_ref.shape[0]
    nheads = q_ref.shape[2] // LANES

    logits = [lbl_ref[d] for d in range(depth)]
    mx = functools.reduce(jnp.maximum, logits)
    es = [jnp.exp(x - mx) for x in logits]
    den = functools.reduce(lambda a, b: a + b, es)
    num = jnp.zeros_like(den)
    for d in range(1, layer + 1):
        num = num + es[d]
    lbs = num / den

    stf_s[...] = jnp.zeros_like(stf_s)
    stb_s[...] = jnp.zeros_like(stb_s)

    def body(n, carry):
        cf = lax.rem(n + nc - 1, nc)
        cb = lax.rem(2 * nc - 2 - n, nc)
        rf = pl.ds(pl.multiple_of(cf * c, c), c)
        rb = pl.ds(pl.multiple_of(cb * c, c), c)
        tri_f = tril_ref[(cf == nc - 1).astype(jnp.int32)]
        tri_b = triu_ref[(cb == nc - 1).astype(jnp.int32)]
        chains = []
        for hh in range(nheads):
            ln = pl.ds(hh * LANES, LANES)
            lanes = slice(hh * LANES, (hh + 1) * LANES)
            chains.append(dict(
                q=q_ref[0, rf, ln].astype(F32), z=zf_ref[0, rf, ln].astype(F32), v=v_ref[0, rf, ln].astype(F32),
                lb=lbs[0:1, lanes], st_ref=stf_s.at[hh], cum_ref=cumf_s.at[hh],
                tri=tri_f, level=lvf_ref[...], rev=False, out=(of_s, rf, ln)))
            chains.append(dict(
                q=q_ref[0, rb, ln].astype(F32), z=zb_ref[0, rb, ln].astype(F32), v=v_ref[0, rb, ln].astype(F32),
                lb=lbs[1:2, lanes], st_ref=stb_s.at[hh], cum_ref=cumb_s.at[hh],
                tri=tri_b, level=lvb_ref[...], rev=True, out=(ob_s, rb, ln)))
        for ch, o in zip(chains, _hgrn_chunks(chains)):
            dst, rows, ln = ch["out"]
            dst[rows, ln] = o
        return carry

    lax.fori_loop(0, nc, body, 0)

    def epilogue(n, carry):
        rows = pl.ds(pl.multiple_of(n * c, c), c)
        pos = n * c + lax.broadcasted_iota(jnp.int32, (c, LANES), 0)
        for hh in range(nheads):
            ln = pl.ds(hh * LANES, LANES)
            o = of_s[rows, ln] + ob_s[rows, ln]
            ms = jnp.mean(o * o, axis=-1, keepdims=True)
            y = o * lax.rsqrt(ms + NORM_EPS) * gain_ref[...] * _silu(gate_ref[0, rows, ln].astype(F32))
            o_ref[0, rows, ln] = jnp.where(pos < seq_len, y, 0.0).astype(o_ref.dtype)
        return carry

    lax.fori_loop(0, nc, epilogue, 0)


def _hgrn(p3d, lb_logits, gain, layer, seq_len):
    b, lp, _ = p3d.shape
    c = HG_CHUNK
    depth = lb_logits.shape[0]
    hps = HG_HEADS_PER_STEP
    wid = hps * LANES
    ngrp = HG_HEADS // hps
    tril, triu, lvf, lvb = _hgrn_constants(seq_len - (lp - c))
    col = lambda g: pl.BlockSpec((1, lp, wid), lambda i, h, g=g: (i, 0, g * ngrp + h))
    const = lambda shape: pl.BlockSpec(shape, lambda i, h: (0,) * len(shape))
    return pl.pallas_call(
        functools.partial(_hgrn_kernel, layer=layer, seq_len=seq_len),
        out_shape=jax.ShapeDtypeStruct((b, lp, HG_HEADS * HG_VDIM), BF16),
        grid=(b, ngrp),
        in_specs=[
            col(0), col(1), col(2), col(3), col(4),
            pl.BlockSpec((depth, 2, wid), lambda i, h: (0, 0, h)),
            const((1, HG_VDIM)),
            const((2, c, c)), const((2, c, c)), const((c, c)), const((c, c)),
        ],
        out_specs=pl.BlockSpec((1, lp, wid), lambda i, h: (i, 0, h)),
        scratch_shapes=[
            pltpu.VMEM((lp, wid), F32), pltpu.VMEM((lp, wid), F32),
            pltpu.VMEM((hps, HG_VDIM, HG_KDIM), F32), pltpu.VMEM((hps, HG_VDIM, HG_KDIM), F32),
            pltpu.VMEM((hps, c, HG_KDIM), F32), pltpu.VMEM((hps, c, HG_KDIM), F32),
        ],
        compiler_params=pltpu.CompilerParams(
            dimension_semantics=("parallel", "parallel"), vmem_limit_bytes=VMEM_LIMIT),
        name="hgrn2",
    )(p3d, p3d, p3d, p3d, p3d, lb_logits, gain.reshape(1, HG_VDIM), tril, triu, lvf, lvb)


def _rope_tables(positions):
    half = AT_HEAD_DIM // 2
    inv = ROPE_THETA ** (-jnp.arange(0, AT_HEAD_DIM, 2, dtype=F32) / AT_HEAD_DIM)
    ang = positions.astype(F32)[:, None] * inv[None, :]
    cos, sin = jnp.cos(ang), jnp.sin(ang)
    reps = LANES // half
    cos_t = jnp.tile(cos, (1, reps))
    sign = jnp.where((jnp.arange(LANES) % AT_HEAD_DIM) < half, -1.0, 1.0).astype(F32)
    sin_t = jnp.tile(sin, (1, reps)) * sign[None, :]
    return cos_t, sin_t


def _rope(t, cos_t, sin_t):
    half = AT_HEAD_DIM // 2
    lane = lax.broadcasted_iota(jnp.int32, t.shape, 1)
    first = (lane % AT_HEAD_DIM) < half
    partner = jnp.where(first, pltpu.roll(t, LANES - half, axis=1), pltpu.roll(t, half, axis=1))
    return t * cos_t + partner * sin_t


AT_TQ_CAP = 704


def _attn_tile(seq_len):
    best = None
    for t in range(16, min(seq_len, AT_TQ_CAP) + 1, 16):
        if seq_len % t == 0:
            best = t
    assert best is not None, seq_len
    return best


def _attn_kernel(q_ref, k_ref, v_ref, gate_ref, cos_ref, sin_ref, lamv_ref, gain_ref, o_ref,
                 k_s, kt_s, vaug_s, vt_s, s_s, m_s, *, tq, lam_init, seq_len):
    lp = k_ref.shape[1]
    nq = seq_len // tq
    nmain = lp - LANES
    ntail = seq_len - nmain
    assert 0 < ntail <= LANES

    tk = _pick_tile(nmain, AT_TQ_CAP, LANES)
    for n in range(nmain // tk):
        rows = pl.ds(n * tk, tk)
        kr = _rope(k_ref[0, rows, :].astype(F32), cos_ref[rows, :], sin_ref[rows, :])
        lane = lax.broadcasted_iota(jnp.int32, (tk, LANES), 1)
        for c in range(2):
            in_half = (lane < AT_HEAD_DIM) if c == 0 else (lane >= AT_HEAD_DIM)
            k_s[c, rows, :] = jnp.where(in_half, kr, 0.0).astype(BF16)
        vaug_s[rows, 0:LANES] = v_ref[0, rows, :]
        vaug_s[rows, LANES:2 * LANES] = jnp.ones((tk, LANES), BF16)
    rows = pl.ds(nmain, LANES)
    kr = _rope(k_ref[0, rows, :].astype(F32), cos_ref[rows, :], sin_ref[rows, :])
    lane = lax.broadcasted_iota(jnp.int32, (LANES, LANES), 1)
    kt_s[0:LANES, :] = jnp.where(lane < AT_HEAD_DIM, kr, 0.0).astype(BF16)
    kt_s[LANES:2 * LANES, :] = jnp.where(lane >= AT_HEAD_DIM, kr, 0.0).astype(BF16)
    vt = v_ref[0, rows, :]
    zeros = jnp.zeros((LANES, LANES), BF16)
    vt_s[0:LANES, 0:LANES] = vt
    vt_s[0:LANES, LANES:2 * LANES] = zeros
    vt_s[LANES:2 * LANES, 0:LANES] = zeros
    vt_s[LANES:2 * LANES, LANES:2 * LANES] = vt

    lv = lamv_ref[...]
    lam = (jnp.exp(jnp.sum(lv[0:1] * lv[1:2], axis=-1, keepdims=True))
           - jnp.exp(jnp.sum(lv[2:3] * lv[3:4], axis=-1, keepdims=True)) + lam_init)

    def scores(j, slot):
        rows = pl.ds(j * tq, tq)
        q = _rope(q_ref[0, rows, :].astype(F32), cos_ref[rows, :], sin_ref[rows, :])
        q = q * (AT_HEAD_DIM ** -0.5 * LOG2E)
        lane = lax.broadcasted_iota(jnp.int32, (tq, LANES), 1)
        tail = _dot_nt(q.astype(BF16), kt_s[...])
        for c in range(2):
            in_half = (lane < AT_HEAD_DIM) if c == 0 else (lane >= AT_HEAD_DIM)
            s = _dot_nt(jnp.where(in_half, q, 0.0).astype(BF16), k_s[c])
            st = jnp.where(lane < ntail, tail[:, c * LANES:(c + 1) * LANES], KEY_MASK)
            s_s[slot, c, :, 0:nmain] = s
            s_s[slot, c, :, nmain:lp] = st
            m_s[slot, c] = jnp.maximum(jnp.max(s, axis=-1, keepdims=True),
                                       jnp.max(st, axis=-1, keepdims=True))

    def outputs(j, slot):
        rows = pl.ds(j * tq, tq)
        mains, tails = [], []
        for c in range(2):
            m = m_s[slot, c]
            p = jnp.exp2(s_s[slot, c, :, 0:nmain] - m).astype(BF16)
            mains.append(_dot(p, vaug_s[...]))
            tails.append(jnp.exp2(s_s[slot, c, :, nmain:lp] - m).astype(BF16))
        ot = _dot(jnp.concatenate(tails, axis=1), vt_s[...])
        outs = []
        for c in range(2):
            lt = jnp.sum(tails[c].astype(F32), axis=-1, keepdims=True)
            outs.append((mains[c][:, 0:LANES] + ot[:, c * LANES:(c + 1) * LANES])
                        / (mains[c][:, LANES:2 * LANES] + lt))
        o = outs[0] - lam * outs[1]
        ms = jnp.mean(o * o, axis=-1, keepdims=True)
        y = (o * lax.rsqrt(ms + NORM_EPS) * gain_ref[...] * (1.0 - lam_init)
             * _silu(gate_ref[0, rows, :].astype(F32)))
        o_ref[0, rows, :] = y.astype(o_ref.dtype)

    scores(0, 0)
    for j in range(nq):
        if j + 1 < nq:
            scores(j + 1, (j + 1) % 2)
        outputs(j, j % 2)
    if seq_len < lp:
        o_ref[0, seq_len:lp, :] = jnp.zeros((lp - seq_len, LANES), o_ref.dtype)


def _attn(p3d, lam_vecs, gain, layer, seq_len, cos_t, sin_t):
    b, lp, _ = p3d.shape
    tq = _attn_tile(seq_len)
    lam_init = 0.8 - 0.6 * math.exp(-0.3 * layer)
    g0 = 5 * AT_HEADS
    col = lambda g: pl.BlockSpec((1, lp, LANES), lambda i, h, g=g: (i, 0, g0 + g * AT_HEADS + h))
    const = lambda shape: pl.BlockSpec(shape, lambda i, h: (0,) * len(shape))
    return pl.pallas_call(
        functools.partial(_attn_kernel, tq=tq, lam_init=lam_init, seq_len=seq_len),
        out_shape=jax.ShapeDtypeStruct((b, lp, AT_HEADS * 2 * AT_HEAD_DIM), BF16),
        grid=(b, AT_HEADS),
        in_specs=[
            col(0), col(1), col(2), col(3),
            const((lp, LANES)), const((lp, LANES)),
            const((4, AT_HEAD_DIM)), const((1, LANES)),
        ],
        out_specs=pl.BlockSpec((1, lp, LANES), lambda i, h: (i, 0, h)),
        scratch_shapes=[
            pltpu.VMEM((2, lp - LANES, LANES), BF16),
            pltpu.VMEM((2 * LANES, LANES), BF16),
            pltpu.VMEM((lp - LANES, 2 * LANES), BF16),
            pltpu.VMEM((2 * LANES, 2 * LANES), BF16),
            pltpu.VMEM((2, 2, tq, lp), F32),
            pltpu.VMEM((2, 2, tq, 1), F32),
        ],
        compiler_params=pltpu.CompilerParams(
            dimension_semantics=("parallel", "parallel"), vmem_limit_bytes=VMEM_LIMIT),
        name="diff_attn",
    )(p3d, p3d, p3d, p3d, cos_t, sin_t, lam_vecs, gain.reshape(1, LANES))


def _outproj_kernel(oh_ref, oa_ref, w_ref, h_ref, fg_ref, out_ref, *, final):
    wh = oh_ref.shape[1]
    acc = h_ref[...] + _dot(oh_ref[...], w_ref[0:wh, :]) + _dot(oa_ref[...], w_ref[wh:, :])
    if final:
        ms = jnp.mean(acc * acc, axis=-1, keepdims=True)
        acc = acc * lax.rsqrt(ms + NORM_EPS) * fg_ref[...]
    out_ref[...] = acc


def _outproj(o_h, o_a, w_bf16, layer, h3d, final_g, rows, final):
    b, _, d = h3d.shape
    wh, wa = o_h.shape[2], o_a.shape[2]
    tm = _pick_tile(rows, OUTPROJ_TM_CAP, 16)
    row_block = lambda w: pl.BlockSpec((None, tm, w), lambda i, j: (i, j, 0))
    return pl.pallas_call(
        functools.partial(_outproj_kernel, final=final),
        out_shape=jax.ShapeDtypeStruct((b, rows, d), F32),
        grid=(b, rows // tm),
        in_specs=[
            row_block(wh), row_block(wa),
            pl.BlockSpec((None, wh + wa, d), lambda i, j: (layer, 0, 0)),
            row_block(d),
            pl.BlockSpec((1, d), lambda i, j: (0, 0)),
        ],
        out_specs=row_block(d),
        compiler_params=pltpu.CompilerParams(
            dimension_semantics=("parallel", "parallel"), vmem_limit_bytes=VMEM_LIMIT),
        name="outproj",
    )(o_h, o_a, w_bf16, h3d, final_g.reshape(1, d))


def kernel(x, meta_tokens, norm_g, w_in, hg_lb_logits, hg_norm_g, diff_lambda, diff_subln_g,
           w_out, final_norm_g):
    bsz, seq, d = x.shape
    depth = w_in.shape[0]
    seq_len = N_META + seq
    lp = seq + HG_CHUNK
    assert seq % HG_CHUNK == 0 and N_META <= HG_CHUNK
    assert w_in.shape[2] == N_GROUPS * HG_HEADS * LANES

    tail = jnp.concatenate([meta_tokens.astype(x.dtype), jnp.zeros((lp - seq_len, d), x.dtype)])
    positions = jnp.concatenate([jnp.arange(N_META, seq_len), jnp.arange(N_META),
                                 jnp.zeros((lp - seq_len,), jnp.int32)])
    cos_t, sin_t = _rope_tables(positions)
    w_out_bf = w_out.astype(BF16)

    for l in range(depth):
        final = l == depth - 1
        if l == 0:
            p, h = _norm_inproj(x, tail, norm_g[l], w_in, l, seq_len, emit_h=True)
        else:
            p, = _norm_inproj(h, h, norm_g[l], w_in, l, seq_len, emit_h=False)
        o_h = _hgrn(p, hg_lb_logits, hg_norm_g[l], l, seq_len)
        o_a = _attn(p, diff_lambda[l], diff_subln_g[l], l, seq_len, cos_t, sin_t)
        h = _outproj(o_h, o_a, w_out_bf, l, h, final_norm_g, seq if final else lp, final)
    return h
```

```python
import functools
import math

import numpy as np
import jax
import jax.numpy as jnp
from jax import lax
from jax.experimental import pallas as pl
from jax.experimental.pallas import tpu as pltpu

F32 = jnp.float32
BF16 = jnp.bfloat16

N_META = 16
HG_HEADS = 8
HG_KDIM = 128
HG_VDIM = 128
AT_HEADS = 8
AT_HEAD_DIM = 64
ROPE_THETA = 10000.0
NORM_EPS = 1e-6
F_FLOOR = 1e-30
N_GROUPS = 9
LOG2E = math.log2(math.e)

LANES = 128
SUBLANES = 8
VMEM_LIMIT = 56 * 1024 * 1024

HG_CHUNK = 128
HG_LEVELS = (64, 32, 16, 8, 4, 2, 1)
HG_DIAG_LEVEL = len(HG_LEVELS)
HG_HEADS_PER_STEP = 4

KEY_MASK = -1e30
INPROJ_TN_CAP = 1024
OUTPROJ_TM_CAP = 576


def _pick_tile(n, cap, mult):
    best = None
    for t in range(mult, min(n, cap) + 1, mult):
        if n % t == 0:
            best = t
    assert best is not None, (n, cap, mult)
    return best


def _dot(a, b):
    return jnp.dot(a, b, preferred_element_type=F32)


def _dot_nt(a, b):
    return lax.dot_general(a, b, (((1,), (1,)), ((), ())), preferred_element_type=F32)


def _silu(g):
    return g * (1.0 / (1.0 + jnp.exp(-g)))


def _norm_inproj_kernel(body_ref, tail_ref, g_ref, w_ref, *rest, emit_h, nreal):
    if emit_h:
        body_col_ref, tail_col_ref, p_ref, h_ref, u_ref = rest
    else:
        p_ref, u_ref = rest
    nbody = body_ref.shape[0]

    @pl.when(pl.program_id(1) == 0)
    def _():
        for src, rows in ((body_ref, slice(0, nbody)), (tail_ref, slice(nbody, None))):
            x = src[...]
            ms = jnp.mean(x * x, axis=-1, keepdims=True)
            u_ref[rows, :] = (x * lax.rsqrt(ms + NORM_EPS) * g_ref[...]).astype(BF16)

    if emit_h:
        h_ref[0:nbody, :] = body_col_ref[...]
        h_ref[nbody:, :] = tail_col_ref[...]
    p_ref[0:nreal, :] = _dot(u_ref[0:nreal, :], w_ref[...].astype(BF16)).astype(BF16)
    p_ref[nreal:, :] = jnp.zeros((p_ref.shape[0] - nreal, p_ref.shape[1]), BF16)


def _norm_inproj(body, tail, g, w, layer, nreal, emit_h):
    b, d = body.shape[0], body.shape[2]
    ntail = HG_CHUNK
    nbody = body.shape[1] if tail.ndim == 2 else body.shape[1] - ntail
    lp = nbody + ntail
    n = w.shape[2]
    tn = _pick_tile(n, INPROJ_TN_CAP, LANES)
    ncol = d // LANES
    if tail.ndim == 2:
        tail_spec = pl.BlockSpec((ntail, d), lambda i, j: (0, 0))
    else:
        tail_spec = pl.BlockSpec((None, ntail, d), lambda i, j: (i, nbody // ntail, 0))
    in_specs = [
        pl.BlockSpec((None, nbody, d), lambda i, j: (i, 0, 0)),
        tail_spec,
        pl.BlockSpec((1, d), lambda i, j: (0, 0)),
        pl.BlockSpec((None, d, tn), lambda i, j: (layer, 0, j)),
    ]
    operands = [body, tail, g.reshape(1, d), w]
    out_shape = [jax.ShapeDtypeStruct((b, lp, n), BF16)]
    out_specs = [pl.BlockSpec((None, lp, tn), lambda i, j: (i, 0, j))]
    if emit_h:
        assert tail.ndim == 2 and n // tn >= ncol
        col = lambda j: jnp.minimum(j, ncol - 1)
        in_specs += [pl.BlockSpec((None, nbody, LANES), lambda i, j: (i, 0, col(j))),
                     pl.BlockSpec((ntail, LANES), lambda i, j: (0, col(j)))]
        operands += [body, tail]
        out_shape.append(jax.ShapeDtypeStruct((b, lp, d), F32))
        out_specs.append(pl.BlockSpec((None, lp, LANES), lambda i, j: (i, 0, col(j))))
    return pl.pallas_call(
        functools.partial(_norm_inproj_kernel, emit_h=emit_h, nreal=nreal),
        out_shape=out_shape,
        grid=(b, n // tn),
        in_specs=in_specs,
        out_specs=out_specs,
        scratch_shapes=[pltpu.VMEM((lp, d), BF16)],
        compiler_params=pltpu.CompilerParams(
            dimension_semantics=("parallel", "arbitrary"), vmem_limit_bytes=VMEM_LIMIT),
        name="norm_inproj",
    )(*operands)


def _hgrn_constants(last_valid):
    c = HG_CHUNK
    t = np.arange(c)[:, None]
    s = np.arange(c)[None, :]
    tril = (s <= t).astype(np.float32)
    real = (np.arange(c) < last_valid).astype(np.float32)[None, :]
    tril = np.stack([tril, tril * real])
    triu = np.stack([tril[0].T, tril[0].T * real])
    x = t ^ s
    top = np.where(x > 0, np.floor(np.log2(np.maximum(x, 1))).astype(np.int64), 0)
    half = 1 << top
    level = np.full((c, c), -1, np.int32)
    for li, hh in enumerate(HG_LEVELS):
        level = np.where((t > s) & (half == hh), li, level)
    level = np.where(t == s, HG_DIAG_LEVEL, level).astype(np.int32)
    return (jnp.asarray(tril, BF16), jnp.asarray(triu, BF16),
            jnp.asarray(level), jnp.asarray(level.T))


def _split2(x):
    hi = x.astype(BF16)
    return hi, (x - hi.astype(F32)).astype(BF16)


def _level_operands(q, kin, cum, cum_ref, half, rev):
    c = HG_CHUNK
    blk = 2 * half
    off = half if rev else half - 1
    zeros = jnp.zeros((half, LANES), F32)
    qt, kt = [], []
    for m in range(c // blk):
        ref = cum_ref[pl.ds(m * blk + off, 1), :]
        lo = slice(m * blk, m * blk + half)
        hi = slice(m * blk + half, (m + 1) * blk)
        if rev:
            qt += [q[lo] * jnp.exp2(cum[lo] - ref), zeros]
            kt += [zeros, kin[hi] * jnp.exp2(ref - cum[hi])]
        else:
            kt += [kin[lo] * jnp.exp2(ref - cum[lo]), zeros]
            qt += [zeros, q[hi] * jnp.exp2(cum[hi] - ref)]
    return jnp.concatenate(qt, axis=0), jnp.concatenate(kt, axis=0)


def _level_exponent(cum, cum_ref, half, rev):
    c = HG_CHUNK
    blk = 2 * half
    off = half if rev else half - 1
    parts = []
    assert half < SUBLANES
    sub = lax.broadcasted_iota(jnp.int32, (SUBLANES, LANES), 0)
    for g in range(c // SUBLANES):
        base = g * SUBLANES
        ref = jnp.broadcast_to(cum_ref[pl.ds(base + off, 1), :], (SUBLANES, LANES))
        if blk < SUBLANES:
            hi = jnp.broadcast_to(cum_ref[pl.ds(base + blk + off, 1), :], (SUBLANES, LANES))
            ref = jnp.where(sub < blk, ref, hi)
        parts.append(-jnp.abs(cum[base:base + SUBLANES] - ref))
    return jnp.concatenate(parts, axis=0)


def _hgrn_chunks(chains):
    c = HG_CHUNK
    row = lax.broadcasted_iota(jnp.int32, (c, LANES), 0)
    for ch in chains:
        sig = 1.0 / (1.0 + jnp.exp(-ch["z"]))
        f = ch["lb"] + (1.0 - ch["lb"]) * sig
        ch["kin"] = 1.0 - f
        ch["f"] = jnp.maximum(f, F_FLOOR)
        ch["parts"] = _split2(jnp.log(ch["f"]))
    for ch in chains:
        hi, lo = ch["parts"]
        cum = (_dot(ch["tri"], hi) + _dot(ch["tri"], lo)) * LOG2E
        ch["cum_ref"][...] = cum
        ch["cum"] = cum
        ch["tot"] = cum[0:1] if ch["rev"] else cum[c - 1:c]
    for ch in chains:
        cum, tot = ch["cum"], ch["tot"]
        q_dec = ch["q"] * jnp.exp2(cum)
        k_dec = ch["kin"] * jnp.exp2(tot - cum)
        st = ch["st_ref"][...]
        ch["o"] = _dot_nt(q_dec.astype(BF16), st.astype(BF16))
        ch["v_bf"] = ch["v"].astype(BF16)
        ch["st_ref"][...] = st * jnp.exp2(tot) + _dot(ch["v"].T.astype(BF16), k_dec.astype(BF16))
        ch["att"] = jnp.where(ch["level"] == HG_DIAG_LEVEL,
                              _dot_nt(ch["q"].astype(BF16), ch["kin"].astype(BF16)), 0.0)
    for li, half in enumerate(HG_LEVELS):
        for ch in chains:
            if half >= SUBLANES:
                qt, kt = _level_operands(ch["q"], ch["kin"], ch["cum"], ch["cum_ref"], half, ch["rev"])
            else:
                if half == 1:
                    q_side = (row % 2 == 0) if ch["rev"] else (row % 2 == 1)
                    ex = jnp.where(q_side, ch["f"], 1.0)
                else:
                    ex = jnp.exp2(_level_exponent(ch["cum"], ch["cum_ref"], half, ch["rev"]))
                qt, kt = ch["q"] * ex, ch["kin"] * ex
            a = _dot_nt(qt.astype(BF16), kt.astype(BF16))
            ch["att"] = jnp.where(ch["level"] == li, a, ch["att"])
    return [ch["o"] + _dot(ch["att"].astype(BF16), ch["v_bf"]) for ch in chains]


def _hgrn_kernel(q_ref, zf_ref, zb_ref, v_ref, lbl_ref,
                 tril_ref, triu_ref, lvf_ref, lvb_ref, o_ref,
                 of_s, ob_s, stf_s, stb_s, cumf_s, cumb_s, *, layer, seq_len):
    c = HG_CHUNK
    lp = q_ref.shape[1]
    nc = lp // c
    depth = lbl_ref.shape[0]
    nheads = q_ref.shape[2] // LANES

    logits = [lbl_ref[d] for d in range(depth)]
    mx = functools.reduce(jnp.maximum, logits)
    es = [jnp.exp(x - mx) for x in logits]
    den = functools.reduce(lambda a, b: a + b, es)
    num = jnp.zeros_like(den)
    for d in range(1, layer + 1):
        num = num + es[d]
    lbs = num / den

    stf_s[...] = jnp.zeros_like(stf_s)
    stb_s[...] = jnp.zeros_like(stb_s)

    def body(n, carry):
        cf = lax.rem(n + nc - 1, nc)
        cb = lax.rem(2 * nc - 2 - n, nc)
        rf = pl.ds(pl.multiple_of(cf * c, c), c)
        rb = pl.ds(pl.multiple_of(cb * c, c), c)
        tri_f = tril_ref[(cf == nc - 1).astype(jnp.int32)]
        tri_b = triu_ref[(cb == nc - 1).astype(jnp.int32)]
        chains = []
        for hh in range(nheads):
            ln = pl.ds(hh * LANES, LANES)
            lanes = slice(hh * LANES, (hh + 1) * LANES)
            chains.append(dict(
                q=q_ref[0, rf, ln].astype(F32), z=zf_ref[0, rf, ln].astype(F32), v=v_ref[0, rf, ln].astype(F32),
                lb=lbs[0:1, lanes], st_ref=stf_s.at[hh], cum_ref=cumf_s.at[hh],
                tri=tri_f, level=lvf_ref[...], rev=False, out=(of_s, rf, ln)))
            chains.append(dict(
                q=q_ref[0, rb, ln].astype(F32), z=zb_ref[0, rb, ln].astype(F32), v=v_ref[0, rb, ln].astype(F32),
                lb=lbs[1:2, lanes], st_ref=stb_s.at[hh], cum_ref=cumb_s.at[hh],
                tri=tri_b, level=lvb_ref[...], rev=True, out=(ob_s, rb, ln)))
        for ch, o in zip(chains, _hgrn_chunks(chains)):
            dst, rows, ln = ch["out"]
            dst[rows, ln] = o
        return carry

    lax.fori_loop(0, nc, body, 0)

    def epilogue(n, carry):
        rows = pl.ds(pl.multiple_of(n * c, c), c)
        pos = n * c + lax.broadcasted_iota(jnp.int32, (c, LANES), 0)
        for hh in range(nheads):
            ln = pl.ds(hh * LANES, LANES)
            o = of_s[rows, ln] + ob_s[rows, ln]
            o_ref[0, rows, ln] = jnp.where(pos < seq_len, o, 0.0).astype(o_ref.dtype)
        return carry

    lax.fori_loop(0, nc, epilogue, 0)


def _hgrn(p3d, lb_logits, layer, seq_len):
    b, lp, _ = p3d.shape
    c = HG_CHUNK
    depth = lb_logits.shape[0]
    hps = HG_HEADS_PER_STEP
    wid = hps * LANES
    ngrp = HG_HEADS // hps
    tril, triu, lvf, lvb = _hgrn_constants(seq_len - (lp - c))
    col = lambda g: pl.BlockSpec((1, lp, wid), lambda i, h, g=g: (i, 0, g * ngrp + h))
    const = lambda shape: pl.BlockSpec(shape, lambda i, h: (0,) * len(shape))
    return pl.pallas_call(
        functools.partial(_hgrn_kernel, layer=layer, seq_len=seq_len),
        out_shape=jax.ShapeDtypeStruct((b, lp, HG_HEADS * HG_VDIM), BF16),
        grid=(b, ngrp),
        in_specs=[
            col(0), col(1), col(2), col(3),
            pl.BlockSpec((depth, 2, wid), lambda i, h: (0, 0, h)),
            const((2, c, c)), const((2, c, c)), const((c, c)), const((c, c)),
        ],
        out_specs=pl.BlockSpec((1, lp, wid), lambda i, h: (i, 0, h)),
        scratch_shapes=[
            pltpu.VMEM((lp, wid), F32), pltpu.VMEM((lp, wid), F32),
            pltpu.VMEM((hps, HG_VDIM, HG_KDIM), F32), pltpu.VMEM((hps, HG_VDIM, HG_KDIM), F32),
            pltpu.VMEM((hps, c, HG_KDIM), F32), pltpu.VMEM((hps, c, HG_KDIM), F32),
        ],
        compiler_params=pltpu.CompilerParams(
            dimension_semantics=("parallel", "parallel"), vmem_limit_bytes=VMEM_LIMIT),
        name="hgrn2",
    )(p3d, p3d, p3d, p3d, lb_logits, tril, triu, lvf, lvb)


def _rope_tables(positions):
    half = AT_HEAD_DIM // 2
    inv = ROPE_THETA ** (-jnp.arange(0, AT_HEAD_DIM, 2, dtype=F32) / AT_HEAD_DIM)
    ang = positions.astype(F32)[:, None] * inv[None, :]
    cos, sin = jnp.cos(ang), jnp.sin(ang)
    reps = LANES // half
    cos_t = jnp.tile(cos, (1, reps))
    sign = jnp.where((jnp.arange(LANES) % AT_HEAD_DIM) < half, -1.0, 1.0).astype(F32)
    sin_t = jnp.tile(sin, (1, reps)) * sign[None, :]
    return cos_t, sin_t


def _rope(t, cos_t, sin_t):
    half = AT_HEAD_DIM // 2
    lane = lax.broadcasted_iota(jnp.int32, t.shape, 1)
    first = (lane % AT_HEAD_DIM) < half
    partner = jnp.where(first, pltpu.roll(t, LANES - half, axis=1), pltpu.roll(t, half, axis=1))
    return t * cos_t + partner * sin_t


AT_TQ_CAP = 704


def _attn_tile(seq_len):
    best = None
    for t in range(16, min(seq_len, AT_TQ_CAP) + 1, 16):
        if seq_len % t == 0:
            best = t
    assert best is not None, seq_len
    return best


def _attn_kernel(q_ref, k_ref, v_ref, gate_ref, cos_ref, sin_ref, lamv_ref, gain_ref, o_ref,
                 k_s, kt_s, vaug_s, vt_s, s_s, m_s, *, tq, lam_init, seq_len):
    lp = k_ref.shape[1]
    nq = seq_len // tq
    nmain = lp - LANES
    ntail = seq_len - nmain
    assert 0 < ntail <= LANES

    tk = _pick_tile(nmain, AT_TQ_CAP, LANES)
    for n in range(nmain // tk):
        rows = pl.ds(n * tk, tk)
        kr = _rope(k_ref[0, rows, :].astype(F32), cos_ref[rows, :], sin_ref[rows, :])
        lane = lax.broadcasted_iota(jnp.int32, (tk, LANES), 1)
        for c in range(2):
            in_half = (lane < AT_HEAD_DIM) if c == 0 else (lane >= AT_HEAD_DIM)
            k_s[c, rows, :] = jnp.where(in_half, kr, 0.0).astype(BF16)
        vaug_s[rows, 0:LANES] = v_ref[0, rows, :]
        vaug_s[rows, LANES:2 * LANES] = jnp.ones((tk, LANES), BF16)
    rows = pl.ds(nmain, LANES)
    kr = _rope(k_ref[0, rows, :].astype(F32), cos_ref[rows, :], sin_ref[rows, :])
    lane = lax.broadcasted_iota(jnp.int32, (LANES, LANES), 1)
    kt_s[0:LANES, :] = jnp.where(lane < AT_HEAD_DIM, kr, 0.0).astype(BF16)
    kt_s[LANES:2 * LANES, :] = jnp.where(lane >= AT_HEAD_DIM, kr, 0.0).astype(BF16)
    vt = v_ref[0, rows, :]
    zeros = jnp.zeros((LANES, LANES), BF16)
    vt_s[0:LANES, 0:LANES] = vt
    vt_s[0:LANES, LANES:2 * LANES] = zeros
    vt_s[LANES:2 * LANES, 0:LANES] = zeros
    vt_s[LANES:2 * LANES, LANES:2 * LANES] = vt

    lv = lamv_ref[...]
    lam = (jnp.exp(jnp.sum(lv[0:1] * lv[1:2], axis=-1, keepdims=True))
           - jnp.exp(jnp.sum(lv[2:3] * lv[3:4], axis=-1, keepdims=True)) + lam_init)

    def scores(j, slot):
        rows = pl.ds(j * tq, tq)
        q = _rope(q_ref[0, rows, :].astype(F32), cos_ref[rows, :], sin_ref[rows, :])
        q = q * (AT_HEAD_DIM ** -0.5 * LOG2E)
        lane = lax.broadcasted_iota(jnp.int32, (tq, LANES), 1)
        tail = _dot_nt(q.astype(BF16), kt_s[...])
        for c in range(2):
            in_half = (lane < AT_HEAD_DIM) if c == 0 else (lane >= AT_HEAD_DIM)
            s = _dot_nt(jnp.where(in_half, q, 0.0).astype(BF16), k_s[c])
            st = jnp.where(lane < ntail, tail[:, c * LANES:(c + 1) * LANES], KEY_MASK)
            s_s[slot, c, :, 0:nmain] = s
            s_s[slot, c, :, nmain:lp] = st
            m_s[slot, c] = jnp.maximum(jnp.max(s, axis=-1, keepdims=True),
                                       jnp.max(st, axis=-1, keepdims=True))

    def outputs(j, slot):
        rows = pl.ds(j * tq, tq)
        mains, tails = [], []
        for c in range(2):
            m = m_s[slot, c]
            p = jnp.exp2(s_s[slot, c, :, 0:nmain] - m).astype(BF16)
            mains.append(_dot(p, vaug_s[...]))
            tails.append(jnp.exp2(s_s[slot, c, :, nmain:lp] - m).astype(BF16))
        ot = _dot(jnp.concatenate(tails, axis=1), vt_s[...])
        outs = []
        for c in range(2):
            lt = jnp.sum(tails[c].astype(F32), axis=-1, keepdims=True)
            outs.append((mains[c][:, 0:LANES] + ot[:, c * LANES:(c + 1) * LANES])
                        / (mains[c][:, LANES:2 * LANES] + lt))
        o = outs[0] - lam * outs[1]
        ms = jnp.mean(o * o, axis=-1, keepdims=True)
        y = (o * lax.rsqrt(ms + NORM_EPS) * gain_ref[...] * (1.0 - lam_init)
             * _silu(gate_ref[0, rows, :].astype(F32)))
        o_ref[0, rows, :] = y.astype(o_ref.dtype)

    scores(0, 0)
    for j in range(nq):
        if j + 1 < nq:
            scores(j + 1, (j + 1) % 2)
        outputs(j, j % 2)
    if seq_len < lp:
        o_ref[0, seq_len:lp, :] = jnp.zeros((lp - seq_len, LANES), o_ref.dtype)


def _attn(p3d, lam_vecs, gain, layer, seq_len, cos_t, sin_t):
    b, lp, _ = p3d.shape
    tq = _attn_tile(seq_len)
    lam_init = 0.8 - 0.6 * math.exp(-0.3 * layer)
    g0 = 5 * AT_HEADS
    col = lambda g: pl.BlockSpec((1, lp, LANES), lambda i, h, g=g: (i, 0, g0 + g * AT_HEADS + h))
    const = lambda shape: pl.BlockSpec(shape, lambda i, h: (0,) * len(shape))
    return pl.pallas_call(
        functools.partial(_attn_kernel, tq=tq, lam_init=lam_init, seq_len=seq_len),
        out_shape=jax.ShapeDtypeStruct((b, lp, AT_HEADS * 2 * AT_HEAD_DIM), BF16),
        grid=(b, AT_HEADS),
        in_specs=[
            col(0), col(1), col(2), col(3),
            const((lp, LANES)), const((lp, LANES)),
            const((4, AT_HEAD_DIM)), const((1, LANES)),
        ],
        out_specs=pl.BlockSpec((1, lp, LANES), lambda i, h: (i, 0, h)),
        scratch_shapes=[
            pltpu.VMEM((2, lp - LANES, LANES), BF16),
            pltpu.VMEM((2 * LANES, LANES), BF16),
            pltpu.VMEM((lp - LANES, 2 * LANES), BF16),
            pltpu.VMEM((2 * LANES, 2 * LANES), BF16),
            pltpu.VMEM((2, 2, tq, lp), F32),
            pltpu.VMEM((2, 2, tq, 1), F32),
        ],
        compiler_params=pltpu.CompilerParams(
            dimension_semantics=("parallel", "parallel"), vmem_limit_bytes=VMEM_LIMIT),
        name="diff_attn",
    )(p3d, p3d, p3d, p3d, cos_t, sin_t, lam_vecs, gain.reshape(1, LANES))


def _outproj_kernel(oh_ref, hgate_ref, hgain_ref, oa_ref, w_ref, h_ref, fg_ref, out_ref, *, final):
    wh = oh_ref.shape[1]
    heads = []
    for hh in range(wh // HG_VDIM):
        lanes = slice(hh * HG_VDIM, (hh + 1) * HG_VDIM)
        o = oh_ref[:, lanes].astype(F32)
        ms = jnp.mean(o * o, axis=-1, keepdims=True)
        y = o * lax.rsqrt(ms + NORM_EPS) * hgain_ref[...] * _silu(hgate_ref[:, lanes].astype(F32))
        heads.append(y.astype(BF16))
    oh = jnp.concatenate(heads, axis=1)
    acc = h_ref[...] + _dot(oh, w_ref[0:wh, :]) + _dot(oa_ref[...], w_ref[wh:, :])
    if final:
        ms = jnp.mean(acc * acc, axis=-1, keepdims=True)
        acc = acc * lax.rsqrt(ms + NORM_EPS) * fg_ref[...]
    out_ref[...] = acc


def _outproj(o_h, p3d, hgain, o_a, w_bf16, layer, h3d, final_g, rows, final):
    b, _, d = h3d.shape
    wh, wa = o_h.shape[2], o_a.shape[2]
    tm = _pick_tile(rows, OUTPROJ_TM_CAP, 16)
    row_block = lambda w: pl.BlockSpec((None, tm, w), lambda i, j: (i, j, 0))
    hgate_group = 4
    return pl.pallas_call(
        functools.partial(_outproj_kernel, final=final),
        out_shape=jax.ShapeDtypeStruct((b, rows, d), F32),
        grid=(b, rows // tm),
        in_specs=[
            row_block(wh),
            pl.BlockSpec((None, tm, wh), lambda i, j: (i, j, hgate_group)),
            pl.BlockSpec((1, HG_VDIM), lambda i, j: (0, 0)),
            row_block(wa),
            pl.BlockSpec((None, wh + wa, d), lambda i, j: (layer, 0, 0)),
            row_block(d),
            pl.BlockSpec((1, d), lambda i, j: (0, 0)),
        ],
        out_specs=row_block(d),
        compiler_params=pltpu.CompilerParams(
            dimension_semantics=("parallel", "parallel"), vmem_limit_bytes=VMEM_LIMIT),
        name="outproj",
    )(o_h, p3d, hgain.reshape(1, HG_VDIM), o_a, w_bf16, h3d, final_g.reshape(1, d))


def kernel(x, meta_tokens, norm_g, w_in, hg_lb_logits, hg_norm_g, diff_lambda, diff_subln_g,
           w_out, final_norm_g):
    bsz, seq, d = x.shape
    depth = w_in.shape[0]
    seq_len = N_META + seq
    lp = seq + HG_CHUNK
    assert seq % HG_CHUNK == 0 and N_META <= HG_CHUNK
    assert w_in.shape[2] == N_GROUPS * HG_HEADS * LANES

    tail = jnp.concatenate([meta_tokens.astype(x.dtype), jnp.zeros((lp - seq_len, d), x.dtype)])
    positions = jnp.concatenate([jnp.arange(N_META, seq_len), jnp.arange(N_META),
                                 jnp.zeros((lp - seq_len,), jnp.int32)])
    cos_t, sin_t = _rope_tables(positions)
    w_out_bf = w_out.astype(BF16)

    for l in range(depth):
        final = l == depth - 1
        if l == 0:
            p, h = _norm_inproj(x, tail, norm_g[l], w_in, l, seq_len, emit_h=True)
        else:
            p, = _norm_inproj(h, h, norm_g[l], w_in, l, seq_len, emit_h=False)
        o_h = _hgrn(p, hg_lb_logits, l, seq_len)
        o_a = _attn(p, diff_lambda[l], diff_subln_g[l], l, seq_len, cos_t, sin_t)
        h = _outproj(o_h, p, hg_norm_g[l], o_a, w_out_bf, l, h, final_norm_g,
                     seq if final else lp, final)
    return h
```
